```python
import math
import jax, jax.numpy as jnp
from jax import lax
import numpy as np

D_MODEL = 1024
BATCH = 8
SEQ = 2048
DEPTH = 1

CHUNK = 64
D_MIX = D_MODEL
D_SSM = D_MIX // 2
SSM_GROUP = 16
N_SSM_GROUPS = D_SSM // SSM_GROUP
SSM_STATE = 64
D_CONV = D_MIX - D_SSM
CONV_WIDTH = 31
D_IN_PROJ = D_SSM + 2 * D_CONV
PEER_HEADS = 8
PEER_KEYS = 128
PEER_EXPERTS = PEER_KEYS * PEER_KEYS
PEER_TOPK = 16
PEER_DKEY = 256
PEER_HALF = PEER_DKEY // 2
PEER_BLOCK = 128
EPS = 1e-6
DT_MIN = 0.001
DT_MAX = 0.1

kernel_name = "hybrid_s5_conformer_peer_block"


def rmsnorm(x, g):
    xf = x.astype(jnp.float32)
    out = xf * lax.rsqrt(jnp.mean(xf * xf, axis=-1, keepdims=True) + EPS) * g.astype(jnp.float32)
    return out.astype(x.dtype)


def s5_mixer(u, lam_re, lam_im, log_dt, b_re, b_im, c_re, c_im, d_skip, glu_w, glu_b):
    bsz, seq, _ = u.shape
    f32 = jnp.float32
    ug = u.astype(f32).reshape(bsz, seq, N_SSM_GROUPS, SSM_GROUP)
    lr = lam_re.astype(f32)
    li = lam_im.astype(f32)
    dt = jnp.exp(log_dt.astype(f32))[:, None]
    mag = jnp.exp(lr * dt)
    ang = li * dt
    lb_re = mag * jnp.cos(ang)
    lb_im = mag * jnp.sin(ang)
    num_re = lb_re - 1.0
    num_im = lb_im
    den = lr * lr + li * li
    f_re = (num_re * lr + num_im * li) / den
    f_im = (num_im * lr - num_re * li) / den
    br = b_re.astype(f32)
    bi = b_im.astype(f32)
    bb_re = f_re[..., None] * br - f_im[..., None] * bi
    bb_im = f_re[..., None] * bi + f_im[..., None] * br
    bu_re = jnp.einsum('blgh,gph->blgp', ug, bb_re)
    bu_im = jnp.einsum('blgh,gph->blgp', ug, bb_im)
    a_re = jnp.broadcast_to(lb_re[None, None], (1, seq, N_SSM_GROUPS, SSM_STATE))
    a_im = jnp.broadcast_to(lb_im[None, None], (1, seq, N_SSM_GROUPS, SSM_STATE))

    def combine(e1, e2):
        a1r, a1i, b1r, b1i = e1
        a2r, a2i, b2r, b2i = e2
        return (a1r * a2r - a1i * a2i,
                a1r * a2i + a1i * a2r,
                a2r * b1r - a2i * b1i + b2r,
                a2r * b1i + a2i * b1r + b2i)

    _, _, s_re, s_im = lax.associative_scan(combine, (a_re, a_im, bu_re, bu_im), axis=1)
    y = (jnp.einsum('blgp,ghp->blgh', s_re, c_re.astype(f32))
         - jnp.einsum('blgp,ghp->blgh', s_im, c_im.astype(f32))
         + d_skip.astype(f32).reshape(N_SSM_GROUPS, SSM_GROUP) * ug)
    y = jax.nn.gelu(y, approximate=False)
    gate = jnp.einsum('blgh,ghk->blgk', y, glu_w.astype(f32)) + glu_b.astype(f32).reshape(N_SSM_GROUPS, SSM_GROUP)
    y = y * jax.nn.sigmoid(gate)
    return y.reshape(bsz, seq, D_SSM).astype(u.dtype)


def conformer_conv(v, g, conv_w, conv_b, ln_g, ln_b):
    h = v * jax.nn.sigmoid(g)
    h = lax.conv_general_dilated(
        h, conv_w[:, None, :], window_strides=(1,), padding=[(CONV_WIDTH - 1, 0)],
        dimension_numbers=('NWC', 'WIO', 'NWC'), feature_group_count=D_CONV) + conv_b
    hf = h.astype(jnp.float32)
    mu = jnp.mean(hf, axis=-1, keepdims=True)
    var = jnp.mean(jnp.square(hf - mu), axis=-1, keepdims=True)
    hn = (hf - mu) * lax.rsqrt(var + EPS) * ln_g.astype(jnp.float32) + ln_b.astype(jnp.float32)
    return jax.nn.silu(hn).astype(v.dtype)


def peer_ffn(xn, w_q, sub_k1, sub_k2, u_tab, v_tab):
    bsz, seq, d = xn.shape
    xt = xn.reshape(-1, PEER_BLOCK, d)
    k1 = sub_k1.astype(jnp.float32)
    k2 = sub_k2.astype(jnp.float32)

    def block(xb):
        q = (xb @ w_q).astype(jnp.float32).reshape(PEER_BLOCK, PEER_HEADS, 2, PEER_HALF)
        s1 = jnp.einsum('thk,nk->thn', q[:, :, 0], k1)
        s2 = jnp.einsum('thk,nk->thn', q[:, :, 1], k2)
        v1, i1 = lax.top_k(s1, PEER_TOPK)
        v2, i2 = lax.top_k(s2, PEER_TOPK)
        cand = (v1[..., :, None] + v2[..., None, :]).reshape(PEER_BLOCK, PEER_HEADS, PEER_TOPK * PEER_TOPK)
        sc, ci = lax.top_k(cand, PEER_TOPK)
        e1 = jnp.take_along_axis(i1, ci // PEER_TOPK, axis=-1)
        e2 = jnp.take_along_axis(i2, ci % PEER_TOPK, axis=-1)
        eid = e1 * PEER_KEYS + e2
        gates = jax.nn.softmax(sc, axis=-1)
        u_sel = u_tab[eid]
        act = jax.nn.gelu(jnp.einsum('td,thkd->thk', xb, u_sel).astype(jnp.float32), approximate=False)
        w = (gates * act).astype(xb.dtype)
        return jnp.einsum('thk,thkd->td', w, v_tab[eid])

    out = lax.map(block, xt)
    return out.reshape(bsz, seq, d)


def setup_inputs(seed: int = 0) -> dict:
    key = jax.random.key(seed)
    ks = jax.random.split(key, 24)
    f32 = jnp.float32
    nrm = lambda k, shape, s: (jax.random.normal(k, shape, f32) * s).astype(f32)
    L_, G, P, H = DEPTH, N_SSM_GROUPS, SSM_STATE, SSM_GROUP
    x = jax.random.normal(ks[0], (BATCH, SEQ, D_MODEL), f32)
    norm1_g = 1.0 + nrm(ks[1], (L_, D_MODEL), 0.02)
    w_in = nrm(ks[2], (L_, D_MODEL, D_IN_PROJ), D_MODEL ** -0.5)
    lam_re = -0.5 + nrm(ks[3], (L_, G, P), 0.01)
    lam_im = math.pi * jnp.arange(P, dtype=f32)[None, None, :] + nrm(ks[4], (L_, G, P), 0.01)
    log_dt = jax.random.uniform(ks[5], (L_, G), f32, math.log(DT_MIN), math.log(DT_MAX))
    ssm_b_re = nrm(ks[6], (L_, G, P, H), (2.0 * H) ** -0.5)
    ssm_b_im = nrm(ks[7], (L_, G, P, H), (2.0 * H) ** -0.5)
    ssm_c_re = nrm(ks[8], (L_, G, H, P), (2.0 * P) ** -0.5)
    ssm_c_im = nrm(ks[9], (L_, G, H, P), (2.0 * P) ** -0.5)
    ssm_d = nrm(ks[10], (L_, D_SSM), 1.0)
    glu_w = nrm(ks[11], (L_, G, H, H), H ** -0.5)
    glu_b = nrm(ks[12], (L_, D_SSM), 0.02)
    conv_w = nrm(ks[13], (L_, CONV_WIDTH, D_CONV), CONV_WIDTH ** -0.5)
    conv_b = nrm(ks[14], (L_, D_CONV), 0.02)
    conv_ln_g = 1.0 + nrm(ks[15], (L_, D_CONV), 0.02)
    conv_ln_b = nrm(ks[16], (L_, D_CONV), 0.02)
    w_out = nrm(ks[17], (L_, D_MIX, D_MODEL), D_MIX ** -0.5)
    norm2_g = 1.0 + nrm(ks[18], (L_, D_MODEL), 0.02)
    peer_wq = nrm(ks[19], (L_, D_MODEL, PEER_HEADS * PEER_DKEY), D_MODEL ** -0.5)
    peer_k1 = nrm(ks[20], (L_, PEER_KEYS, PEER_HALF), PEER_HALF ** -0.5)
    peer_k2 = nrm(ks[21], (L_, PEER_KEYS, PEER_HALF), PEER_HALF ** -0.5)
    peer_u = nrm(ks[22], (L_, PEER_EXPERTS, D_MODEL), D_MODEL ** -0.5)
    kv, kf = jax.random.split(ks[23])
    peer_v = nrm(kv, (L_, PEER_EXPERTS, D_MODEL), 0.5)
    final_g = 1.0 + nrm(kf, (D_MODEL,), 0.02)
    return {"x": x, "norm1_g": norm1_g, "w_in": w_in, "lam_re": lam_re, "lam_im": lam_im,
            "log_dt": log_dt, "ssm_b_re": ssm_b_re, "ssm_b_im": ssm_b_im, "ssm_c_re": ssm_c_re,
            "ssm_c_im": ssm_c_im, "ssm_d": ssm_d, "glu_w": glu_w, "glu_b": glu_b,
            "conv_w": conv_w, "conv_b": conv_b, "conv_ln_g": conv_ln_g, "conv_ln_b": conv_ln_b,
            "w_out": w_out, "norm2_g": norm2_g, "peer_wq": peer_wq, "peer_k1": peer_k1,
            "peer_k2": peer_k2, "peer_u": peer_u, "peer_v": peer_v, "final_g": final_g}


def reference(x, norm1_g, w_in, lam_re, lam_im, log_dt, ssm_b_re, ssm_b_im, ssm_c_re, ssm_c_im,
              ssm_d, glu_w, glu_b, conv_w, conv_b, conv_ln_g, conv_ln_b, w_out, norm2_g,
              peer_wq, peer_k1, peer_k2, peer_u, peer_v, final_g):
    for l in range(DEPTH):
        h = rmsnorm(x, norm1_g[l])
        z = h @ w_in[l]
        u_ssm = z[..., :D_SSM]
        v_conv = z[..., D_SSM:D_SSM + D_CONV]
        g_conv = z[..., D_SSM + D_CONV:]
        y_ssm = s5_mixer(u_ssm, lam_re[l], lam_im[l], log_dt[l], ssm_b_re[l], ssm_b_im[l],
                         ssm_c_re[l], ssm_c_im[l], ssm_d[l], glu_w[l], glu_b[l])
        y_conv = conformer_conv(v_conv, g_conv, conv_w[l], conv_b[l], conv_ln_g[l], conv_ln_b[l])
        x = x + jnp.concatenate([y_ssm, y_conv], axis=-1) @ w_out[l]
        x = x + peer_ffn(rmsnorm(x, norm2_g[l]), peer_wq[l], peer_k1[l], peer_k2[l], peer_u[l], peer_v[l])
    return rmsnorm(x, final_g)
```

```python
import functools
import math

import jax
import jax.numpy as jnp
from jax import lax
from jax.experimental import pallas as pl
from jax.experimental.pallas import tpu as pltpu

F32 = jnp.float32
BF16 = jnp.bfloat16

EPS = 1e-6
TOPK = 16
SUBLANES = 8
LANES = 128
VMEM_LIMIT = 56 * 1024 * 1024
NOT_SELECTED = 127.0


def _gelu(x):
    return 0.5 * x * (1.0 + lax.erf(x * (1.0 / math.sqrt(2.0))))


def _sigmoid(x):
    return 1.0 / (1.0 + jnp.exp(-x))


def _rms_scale(x):
    return lax.rsqrt(jnp.mean(x * x, axis=-1, keepdims=True) + EPS)


SCAN_LANES = 256
CONV_ROWS = 32
HIST = 32


def _mixer_kernel(x_ref, g1_ref, win_ref, bbd_ref, cbd_ref, m_ref, ap_ref, d_ref, glu_ref, glub_ref,
                  cw_ref, cb_ref, lng_ref, lnb_ref, wout_ref, o_ref,
                  bu_ref, carry_ref, hbuf_ref, ycat_ref):
    tl = x_ref.shape[0]
    ns = carry_ref.shape[-1]
    ds = d_ref.shape[-1]
    dc = cb_ref.shape[-1]
    width = cw_ref.shape[0]

    @pl.when(pl.program_id(1) == 0)
    def _():
        carry_ref[...] = jnp.zeros(carry_ref.shape, F32)
        hbuf_ref[0:HIST, :] = jnp.zeros((HIST, dc), F32)

    x = x_ref[...]
    h = x * _rms_scale(x) * g1_ref[...]
    z = jnp.dot(h.astype(BF16), win_ref[...], preferred_element_type=F32)
    u = z[:, :ds]
    v = z[:, ds:ds + dc]
    g = z[:, ds + dc:]

    bu_ref[...] = jnp.dot(u.astype(BF16), bbd_ref[...], preferred_element_type=F32)

    for c in range(ns // SCAN_LANES):
        lo = c * SCAN_LANES
        re = slice(lo, lo + SCAN_LANES)
        im = slice(ns + lo, ns + lo + SCAN_LANES)
        levels = [(m_ref[2 * i, :, re], m_ref[2 * i + 1, :, re], 1 << i) for i in range(3)]
        pr = ap_ref[0, :, re]
        pi = ap_ref[1, :, re]

        def body(r, carry, re=re, im=im, levels=levels, pr=pr, pi=pi):
            cr, ci = carry
            rows = pl.ds(pl.multiple_of(r * SUBLANES, SUBLANES), SUBLANES)
            xr = bu_ref[rows, re]
            xi = bu_ref[rows, im]
            for mr, mi, shift in levels:
                sr = pltpu.roll(xr, shift, 0)
                si = pltpu.roll(xi, shift, 0)
                xr, xi = xr + mr * sr - mi * si, xi + mr * si + mi * sr
            xr, xi = xr + pr * cr - pi * ci, xi + pr * ci + pi * cr
            bu_ref[rows, re] = xr
            bu_ref[rows, im] = xi
            last = SUBLANES - 1
            return (jnp.broadcast_to(xr[last:last + 1, :], xr.shape),
                    jnp.broadcast_to(xi[last:last + 1, :], xi.shape))

        cr, ci = lax.fori_loop(0, tl // SUBLANES, body, (carry_ref[0, :, re], carry_ref[1, :, re]))
        carry_ref[0, :, re] = cr
        carry_ref[1, :, re] = ci

    y = jnp.dot(bu_ref[...].astype(BF16), cbd_ref[...], preferred_element_type=F32) + d_ref[...] * u
    yg = _gelu(y)
    gate = jnp.dot(yg.astype(BF16), glu_ref[...], preferred_element_type=F32) + glub_ref[...]
    ycat_ref[:, 0:ds] = (yg * _sigmoid(gate)).astype(BF16)

    hbuf_ref[HIST:HIST + tl, :] = v * _sigmoid(g)
    for r0 in range(0, tl, CONV_ROWS):
        acc = jnp.broadcast_to(cb_ref[...], (CONV_ROWS, dc))
        for k in range(width):
            start = r0 + HIST - (width - 1) + k
            acc = acc + cw_ref[k:k + 1, :] * hbuf_ref[start:start + CONV_ROWS, :]
        mu = jnp.mean(acc, axis=-1, keepdims=True)
        cen = acc - mu
        var = jnp.mean(cen * cen, axis=-1, keepdims=True)
        hn = cen * lax.rsqrt(var + EPS) * lng_ref[...] + lnb_ref[...]
        ycat_ref[r0:r0 + CONV_ROWS, ds:ds + dc] = (hn * _sigmoid(hn)).astype(BF16)
    hbuf_ref[0:HIST, :] = hbuf_ref[tl:tl + HIST, :]

    o_ref[...] = x + jnp.dot(ycat_ref[...], wout_ref[...], preferred_element_type=F32)


def _s5_tables(lam_re, lam_im, log_dt, b_re, b_im, c_re, c_im, glu_w):
    ng, ns = lam_re.shape
    gh = b_re.shape[-1]
    dt = jnp.exp(log_dt)[:, None]
    mag = jnp.exp(lam_re * dt)
    ang = lam_im * dt
    a_re = mag * jnp.cos(ang)
    a_im = mag * jnp.sin(ang)
    num_re = a_re - 1.0
    num_im = a_im
    den = lam_re * lam_re + lam_im * lam_im
    f_re = (num_re * lam_re + num_im * lam_im) / den
    f_im = (num_im * lam_re - num_re * lam_im) / den
    bb_re = f_re[..., None] * b_re - f_im[..., None] * b_im
    bb_im = f_re[..., None] * b_im + f_im[..., None] * b_re
    eye = jnp.eye(ng, dtype=F32)
    to_state = lambda m: jnp.einsum('gph,gk->ghkp', m, eye).reshape(ng * gh, ng * ns)
    from_state = lambda m: jnp.einsum('ghp,gk->gpkh', m, eye).reshape(ng * ns, ng * gh)
    bbd = jnp.concatenate([to_state(bb_re), to_state(bb_im)], axis=1).astype(BF16)
    cbd = jnp.concatenate([from_state(c_re), -from_state(c_im)], axis=0).astype(BF16)
    glu_bd = jnp.einsum('ghk,gm->ghmk', glu_w, eye).reshape(ng * gh, ng * gh).astype(BF16)

    ar = a_re.reshape(1, ng * ns)
    ai = a_im.reshape(1, ng * ns)
    powers = [(ar, ai)]
    for _ in range(SUBLANES - 1):
        qr, qi = powers[-1]
        powers.append((qr * ar - qi * ai, qr * ai + qi * ar))
    row = jnp.arange(SUBLANES)[:, None]
    shift_mult = []
    for shift in (1, 2, 4):
        qr, qi = powers[shift - 1]
        shift_mult.append(jnp.where(row >= shift, qr, 0.0))
        shift_mult.append(jnp.where(row >= shift, qi, 0.0))
    shift_mult = jnp.stack(shift_mult)
    carry_mult = jnp.stack([jnp.concatenate([p[0] for p in powers], axis=0),
                            jnp.concatenate([p[1] for p in powers], axis=0)])
    return bbd, cbd, glu_bd, shift_mult, carry_mult


def _mixer(x, norm1_g, w_in, tables, ssm_d, glu_b, conv_w, conv_b, ln_g, ln_b, w_out, *, tl):
    bsz, seq, d = x.shape
    bbd, cbd, glu_bd, shift_mult, carry_mult = tables
    ds = ssm_d.shape[-1]
    dc = conv_b.shape[-1]
    ns = carry_mult.shape[-1]
    row = lambda a: a.reshape(1, -1).astype(F32)
    const = lambda a: pl.BlockSpec(a.shape, lambda b, t: (0,) * a.ndim)
    operands = [row(norm1_g), w_in.astype(BF16), bbd, cbd, shift_mult, carry_mult, row(ssm_d), glu_bd,
                row(glu_b), conv_w.astype(F32), row(conv_b), row(ln_g), row(ln_b), w_out.astype(BF16)]
    return pl.pallas_call(
        _mixer_kernel,
        out_shape=jax.ShapeDtypeStruct((bsz, seq, d), F32),
        grid=(bsz, seq // tl),
        in_specs=[pl.BlockSpec((None, tl, d), lambda b, t: (b, t, 0))] + [const(a) for a in operands],
        out_specs=pl.BlockSpec((None, tl, d), lambda b, t: (b, t, 0)),
        scratch_shapes=[pltpu.VMEM((tl, 2 * ns), F32),
                        pltpu.VMEM((2, SUBLANES, ns), F32),
                        pltpu.VMEM((tl + HIST, dc), F32),
                        pltpu.VMEM((tl, ds + dc), BF16)],
        compiler_params=pltpu.CompilerParams(dimension_semantics=("arbitrary", "arbitrary"),
                                             vmem_limit_bytes=VMEM_LIMIT),
        name="mixer",
    )(x, *operands)


def _top16(vals, tie):
    width = vals.shape[1]
    slot = lax.broadcasted_iota(jnp.int32, (TOPK, width), 0)
    work = vals
    rank = jnp.full(vals.shape, NOT_SELECTED, F32)
    tops = jnp.zeros((TOPK, width), F32)
    for r in range(TOPK):
        m = jnp.max(work, axis=0, keepdims=True)
        first = jnp.min(jnp.where(work == m, tie, jnp.inf), axis=0, keepdims=True)
        hit = tie == first
        work = jnp.where(hit, -jnp.inf, work)
        rank = jnp.where(hit, float(r), rank)
        tops = jnp.where(slot == r, m, tops)
    return tops, rank


def _route_chunk(s1, s2):
    nk, width = s1.shape
    key_id = lax.broadcasted_iota(jnp.int32, (nk, width), 0).astype(F32)
    v1, r1 = _top16(s1, key_id)
    v2, r2 = _top16(s2, key_id)

    half = TOPK // 2
    pieces = [v1 + v2[j:j + 1, :] for j in range(half)] + [v1[0:1, :] + v2[half:TOPK, :]]
    cand = jnp.concatenate(pieces, axis=0)
    row = lax.broadcasted_iota(jnp.int32, cand.shape, 0)
    flat = jnp.where(row < half * TOPK, (row % TOPK) * TOPK + row // TOPK, row - half * TOPK + half)
    _, rank = _top16(cand, flat.astype(F32))
    sel = jnp.where(rank < float(TOPK), 1.0, 0.0)
    cnt = sel[0:TOPK, :]
    for j in range(1, half):
        cnt = cnt + sel[j * TOPK:(j + 1) * TOPK, :]
    tail = jnp.sum(sel[half * TOPK:, :], axis=0, keepdims=True)
    slot = lax.broadcasted_iota(jnp.int32, (TOPK, width), 0)
    cnt = jnp.where(slot == 0, cnt + tail, cnt)

    e1 = jnp.exp(v1 - v1[0:1, :])
    e2 = jnp.exp(v2 - v2[0:1, :])
    inner = jnp.zeros((TOPK, width), F32)
    for j in range(TOPK):
        inner = inner + jnp.where(cnt > float(j), e2[j:j + 1, :], 0.0)
    inv_z = 1.0 / jnp.sum(e1 * inner, axis=0, keepdims=True)

    cnt_key = jnp.zeros((nk, width), F32)
    for i in range(TOPK):
        cnt_key = cnt_key + jnp.where(r1 == float(i), cnt[i:i + 1, :], 0.0)
    f1 = jnp.exp(s1 - v1[0:1, :]) * inv_z
    f2 = jnp.exp(s2 - v2[0:1, :])
    return r2, f2, cnt_key, f1


def _route_kernel(x2_ref, g2_ref, wqt_ref, k1_ref, k2_ref,
                  xnt_ref, r2_ref, f2_ref, cnt_ref, f1_ref, qt_ref):
    head = pl.program_id(1)
    nk = k1_ref.shape[0]
    half = k1_ref.shape[1]
    tb = x2_ref.shape[0]

    @pl.when(head == 0)
    def _():
        x = x2_ref[...]
        xn = x * _rms_scale(x) * g2_ref[...]
        xnt = xn.T.astype(BF16)
        xnt_ref[...] = xnt
        qt_ref[...] = jnp.dot(wqt_ref[...], xnt, preferred_element_type=F32).astype(BF16)

    base = pl.multiple_of(head * (2 * half), 2 * half)
    s1 = jnp.dot(k1_ref[...], qt_ref[pl.ds(base, half), :], preferred_element_type=F32)
    s2 = jnp.dot(k2_ref[...], qt_ref[pl.ds(base + half, half), :], preferred_element_type=F32)
    for c in range(tb // LANES):
        cols = slice(c * LANES, (c + 1) * LANES)
        r2, f2, cnt_key, f1 = _route_chunk(s1[:, cols], s2[:, cols])
        r2_ref[:, cols] = r2.astype(BF16)
        f2_ref[:, cols] = f2.astype(BF16)
        cnt_ref[:, cols] = cnt_key
        f1_ref[:, cols] = f1
    del nk


def _route(x2, norm2_g, wq_t, k1, k2, *, tb, heads):
    n, d = x2.shape
    nk, half = k1.shape
    tok = lambda dt: jax.ShapeDtypeStruct((heads, nk, n), dt)
    per_head = pl.BlockSpec((None, nk, tb), lambda i, h: (h, 0, i))
    const = lambda a: pl.BlockSpec(a.shape, lambda i, h: (0,) * a.ndim)
    g2 = norm2_g.reshape(1, d).astype(F32)
    return pl.pallas_call(
        _route_kernel,
        out_shape=(jax.ShapeDtypeStruct((d, n), BF16), tok(BF16), tok(BF16), tok(F32), tok(F32)),
        grid=(n // tb, heads),
        in_specs=[pl.BlockSpec((tb, d), lambda i, h: (i, 0)), const(g2), const(wq_t), const(k1), const(k2)],
        out_specs=(pl.BlockSpec((d, tb), lambda i, h: (0, i)), per_head, per_head, per_head, per_head),
        scratch_shapes=[pltpu.VMEM((wq_t.shape[0], tb), BF16)],
        compiler_params=pltpu.CompilerParams(dimension_semantics=("arbitrary", "arbitrary"),
                                             vmem_limit_bytes=VMEM_LIMIT),
        name="route",
    )(x2, g2, wq_t, k1, k2)


def _peer_kernel(xnt_ref, u_ref, vt_ref, r2_ref, f2_ref, cnt_ref, f1_ref, x2_ref, gf_ref, o_ref,
                 acc_ref, at_ref, wt_ref):
    j = pl.program_id(1)
    heads, nk, tm = r2_ref.shape
    eb = u_ref.shape[0]
    blocks = eb // nk

    @pl.when(j == 0)
    def _():
        acc_ref[...] = jnp.zeros(acc_ref.shape, F32)

    at_ref[...] = jnp.dot(u_ref[...], xnt_ref[...], preferred_element_type=F32)
    for b in range(blocks):
        e1 = j * blocks + b
        rows = slice(b * nk, (b + 1) * nk)
        act = _gelu(at_ref[rows, :]).astype(BF16)
        gate = jnp.zeros((nk, tm), BF16)
        for h in range(heads):
            cnt = cnt_ref[h, pl.ds(e1, 1), :].astype(BF16)
            f1 = f1_ref[h, pl.ds(e1, 1), :].astype(BF16)
            gate = gate + jnp.where(r2_ref[h] < cnt, f2_ref[h], jnp.zeros((), BF16)) * f1
        wt_ref[rows, :] = act * gate
    acc_ref[...] += jnp.dot(vt_ref[...], wt_ref[...], preferred_element_type=F32)

    @pl.when(j == pl.num_programs(1) - 1)
    def _():
        y = x2_ref[...] + acc_ref[...].T
        o_ref[...] = y * _rms_scale(y) * gf_ref[...]


def _peer(xnt, u_bf, vt_bf, r2, f2, cnt, f1, x2, final_g, *, tm, eb):
    d, n = xnt.shape
    ne = u_bf.shape[0]
    heads, nk, _ = r2.shape
    gf = final_g.reshape(1, d).astype(F32)
    tables = pl.BlockSpec((heads, nk, tm), lambda i, j: (0, 0, i))
    return pl.pallas_call(
        _peer_kernel,
        out_shape=jax.ShapeDtypeStruct((n, d), F32),
        grid=(n // tm, ne // eb),
        in_specs=[pl.BlockSpec((d, tm), lambda i, j: (0, i)),
                  pl.BlockSpec((eb, d), lambda i, j: (j, 0)),
                  pl.BlockSpec((d, eb), lambda i, j: (0, j)),
                  tables, tables, tables, tables,
                  pl.BlockSpec((tm, d), lambda i, j: (i, 0)),
                  pl.BlockSpec((1, d), lambda i, j: (0, 0))],
        out_specs=pl.BlockSpec((tm, d), lambda i, j: (i, 0)),
        scratch_shapes=[pltpu.VMEM((d, tm), F32), pltpu.VMEM((eb, tm), F32), pltpu.VMEM((eb, tm), BF16)],
        compiler_params=pltpu.CompilerParams(dimension_semantics=("arbitrary", "arbitrary"),
                                             vmem_limit_bytes=VMEM_LIMIT),
        name="peer",
    )(xnt, u_bf, vt_bf, r2, f2, cnt, f1, x2, gf)


def _layer(x, norm1_g, w_in, lam_re, lam_im, log_dt, b_re, b_im, c_re, c_im, ssm_d, glu_w, glu_b,
           conv_w, conv_b, ln_g, ln_b, w_out, norm2_g, wq, k1, k2, u_tab, v_tab, out_g,
           *, tl, tb, tm, eb):
    bsz, seq, d = x.shape
    heads = wq.shape[1] // (2 * k1.shape[1])
    tables = _s5_tables(lam_re, lam_im, log_dt, b_re, b_im, c_re, c_im, glu_w)
    x2 = _mixer(x, norm1_g, w_in, tables, ssm_d, glu_b, conv_w, conv_b, ln_g, ln_b, w_out, tl=tl)
    x2 = x2.reshape(bsz * seq, d)
    xnt, r2, f2, cnt, f1 = _route(x2, norm2_g, wq.T.astype(BF16), k1.astype(BF16), k2.astype(BF16),
                                  tb=tb, heads=heads)
    out = _peer(xnt, u_tab.astype(BF16), v_tab.T.astype(BF16), r2, f2, cnt, f1, x2, out_g, tm=tm, eb=eb)
    return out.reshape(bsz, seq, d)


def kernel(x, norm1_g, w_in, lam_re, lam_im, log_dt, ssm_b_re, ssm_b_im, ssm_c_re, ssm_c_im, ssm_d, glu_w,
           glu_b, conv_w, conv_b, conv_ln_g, conv_ln_b, w_out, norm2_g, peer_wq, peer_k1, peer_k2, peer_u,
           peer_v, final_g):
    depth = w_in.shape[0]
    d = x.shape[-1]
    for l in range(depth):
        last = l == depth - 1
        assert last, "only the single-layer configuration is implemented"
        x = _layer(x, norm1_g[l], w_in[l], lam_re[l], lam_im[l], log_dt[l], ssm_b_re[l], ssm_b_im[l],
                   ssm_c_re[l], ssm_c_im[l], ssm_d[l], glu_w[l], glu_b[l], conv_w[l], conv_b[l],
                   conv_ln_g[l], conv_ln_b[l], w_out[l], norm2_g[l], peer_wq[l], peer_k1[l], peer_k2[l],
                   peer_u[l], peer_v[l], final_g, tl=256, tb=256, tm=512, eb=1024)
    del d
    return x
```

```python
import functools
import math

import jax
import jax.numpy as jnp
from jax import lax
from jax.experimental import pallas as pl
from jax.experimental.pallas import tpu as pltpu

F32 = jnp.float32
BF16 = jnp.bfloat16

EPS = 1e-6
TOPK = 16
SUBLANES = 8
LANES = 128
VMEM_LIMIT = 56 * 1024 * 1024
NOT_SELECTED = 127.0


def _gelu(x):
    return 0.5 * x * (1.0 + lax.erf(x * (1.0 / math.sqrt(2.0))))


def _sigmoid(x):
    return 1.0 / (1.0 + jnp.exp(-x))


def _rms_scale(x):
    return lax.rsqrt(jnp.mean(x * x, axis=-1, keepdims=True) + EPS)


SCAN_LANES = 256
CONV_ROWS = 32
HIST = 32


def _mixer_kernel(x_ref, g1_ref, win_ref, bbd_ref, cbd_ref, m_ref, ap_ref, d_ref, glu_ref, glub_ref,
                  cw_ref, cb_ref, lng_ref, lnb_ref, wout_ref, o_ref,
                  bu_ref, carry_ref, hbuf_ref, ycat_ref):
    tl = x_ref.shape[0]
    ns = carry_ref.shape[-1]
    ds = d_ref.shape[-1]
    dc = cb_ref.shape[-1]
    width = cw_ref.shape[0]

    @pl.when(pl.program_id(1) == 0)
    def _():
        carry_ref[...] = jnp.zeros(carry_ref.shape, F32)
        hbuf_ref[0:HIST, :] = jnp.zeros((HIST, dc), F32)

    x = x_ref[...]
    h = x * _rms_scale(x) * g1_ref[...]
    z = jnp.dot(h.astype(BF16), win_ref[...], preferred_element_type=F32)
    u = z[:, :ds]
    v = z[:, ds:ds + dc]
    g = z[:, ds + dc:]

    bu_ref[...] = jnp.dot(u.astype(BF16), bbd_ref[...], preferred_element_type=F32)

    for c in range(ns // SCAN_LANES):
        lo = c * SCAN_LANES
        re = slice(lo, lo + SCAN_LANES)
        im = slice(ns + lo, ns + lo + SCAN_LANES)
        levels = [(m_ref[2 * i, :, re], m_ref[2 * i + 1, :, re], 1 << i) for i in range(3)]
        pr = ap_ref[0, :, re]
        pi = ap_ref[1, :, re]

        def body(r, carry, re=re, im=im, levels=levels, pr=pr, pi=pi):
            cr, ci = carry
            rows = pl.ds(pl.multiple_of(r * SUBLANES, SUBLANES), SUBLANES)
            xr = bu_ref[rows, re]
            xi = bu_ref[rows, im]
            for mr, mi, shift in levels:
                sr = pltpu.roll(xr, shift, 0)
                si = pltpu.roll(xi, shift, 0)
                xr, xi = xr + mr * sr - mi * si, xi + mr * si + mi * sr
            xr, xi = xr + pr * cr - pi * ci, xi + pr * ci + pi * cr
            bu_ref[rows, re] = xr
            bu_ref[rows, im] = xi
            last = SUBLANES - 1
            return (jnp.broadcast_to(xr[last:last + 1, :], xr.shape),
                    jnp.broadcast_to(xi[last:last + 1, :], xi.shape))

        cr, ci = lax.fori_loop(0, tl // SUBLANES, body, (carry_ref[0, :, re], carry_ref[1, :, re]))
        carry_ref[0, :, re] = cr
        carry_ref[1, :, re] = ci

    y = jnp.dot(bu_ref[...].astype(BF16), cbd_ref[...], preferred_element_type=F32) + d_ref[...] * u
    yg = _gelu(y)
    gate = jnp.dot(yg.astype(BF16), glu_ref[...], preferred_element_type=F32) + glub_ref[...]
    ycat_ref[:, 0:ds] = (yg * _sigmoid(gate)).astype(BF16)

    hbuf_ref[HIST:HIST + tl, :] = v * _sigmoid(g)
    for r0 in range(0, tl, CONV_ROWS):
        acc = jnp.broadcast_to(cb_ref[...], (CONV_ROWS, dc))
        for k in range(width):
            start = r0 + HIST - (width - 1) + k
            acc = acc + cw_ref[k:k + 1, :] * hbuf_ref[start:start + CONV_ROWS, :]
        mu = jnp.mean(acc, axis=-1, keepdims=True)
        cen = acc - mu
        var = jnp.mean(cen * cen, axis=-1, keepdims=True)
        hn = cen * lax.rsqrt(var + EPS) * lng_ref[...] + lnb_ref[...]
        ycat_ref[r0:r0 + CONV_ROWS, ds:ds + dc] = (hn * _sigmoid(hn)).astype(BF16)
    hbuf_ref[0:HIST, :] = hbuf_ref[tl:tl + HIST, :]

    o_ref[...] = x + jnp.dot(ycat_ref[...], wout_ref[...], preferred_element_type=F32)


def _s5_tables(lam_re, lam_im, log_dt, b_re, b_im, c_re, c_im, glu_w):
    ng, ns = lam_re.shape
    gh = b_re.shape[-1]
    dt = jnp.exp(log_dt)[:, None]
    mag = jnp.exp(lam_re * dt)
    ang = lam_im * dt
    a_re = mag * jnp.cos(ang)
    a_im = mag * jnp.sin(ang)
    num_re = a_re - 1.0
    num_im = a_im
    den = lam_re * lam_re + lam_im * lam_im
    f_re = (num_re * lam_re + num_im * lam_im) / den
    f_im = (num_im * lam_re - num_re * lam_im) / den
    bb_re = f_re[..., None] * b_re - f_im[..., None] * b_im
    bb_im = f_re[..., None] * b_im + f_im[..., None] * b_re
    eye = jnp.eye(ng, dtype=F32)
    to_state = lambda m: jnp.einsum('gph,gk->ghkp', m, eye).reshape(ng * gh, ng * ns)
    from_state = lambda m: jnp.einsum('ghp,gk->gpkh', m, eye).reshape(ng * ns, ng * gh)
    bbd = jnp.concatenate([to_state(bb_re), to_state(bb_im)], axis=1).astype(BF16)
    cbd = jnp.concatenate([from_state(c_re), -from_state(c_im)], axis=0).astype(BF16)
    glu_bd = jnp.einsum('ghk,gm->ghmk', glu_w, eye).reshape(ng * gh, ng * gh).astype(BF16)

    ar = a_re.reshape(1, ng * ns)
    ai = a_im.reshape(1, ng * ns)
    powers = [(ar, ai)]
    for _ in range(SUBLANES - 1):
        qr, qi = powers[-1]
        powers.append((qr * ar - qi * ai, qr * ai + qi * ar))
    row = jnp.arange(SUBLANES)[:, None]
    shift_mult = []
    for shift in (1, 2, 4):
        qr, qi = powers[shift - 1]
        shift_mult.append(jnp.where(row >= shift, qr, 0.0))
        shift_mult.append(jnp.where(row >= shift, qi, 0.0))
    shift_mult = jnp.stack(shift_mult)
    carry_mult = jnp.stack([jnp.concatenate([p[0] for p in powers], axis=0),
                            jnp.concatenate([p[1] for p in powers], axis=0)])
    return bbd, cbd, glu_bd, shift_mult, carry_mult


def _mixer(x, norm1_g, w_in, tables, ssm_d, glu_b, conv_w, conv_b, ln_g, ln_b, w_out, *, tl):
    bsz, seq, d = x.shape
    bbd, cbd, glu_bd, shift_mult, carry_mult = tables
    ds = ssm_d.shape[-1]
    dc = conv_b.shape[-1]
    ns = carry_mult.shape[-1]
    row = lambda a: a.reshape(1, -1).astype(F32)
    const = lambda a: pl.BlockSpec(a.shape, lambda b, t: (0,) * a.ndim)
    operands = [row(norm1_g), w_in.astype(BF16), bbd, cbd, shift_mult, carry_mult, row(ssm_d), glu_bd,
                row(glu_b), conv_w.astype(F32), row(conv_b), row(ln_g), row(ln_b), w_out.astype(BF16)]
    return pl.pallas_call(
        _mixer_kernel,
        out_shape=jax.ShapeDtypeStruct((bsz, seq, d), F32),
        grid=(bsz, seq // tl),
        in_specs=[pl.BlockSpec((None, tl, d), lambda b, t: (b, t, 0))] + [const(a) for a in operands],
        out_specs=pl.BlockSpec((None, tl, d), lambda b, t: (b, t, 0)),
        scratch_shapes=[pltpu.VMEM((tl, 2 * ns), F32),
                        pltpu.VMEM((2, SUBLANES, ns), F32),
                        pltpu.VMEM((tl + HIST, dc), F32),
                        pltpu.VMEM((tl, ds + dc), BF16)],
        compiler_params=pltpu.CompilerParams(dimension_semantics=("arbitrary", "arbitrary"),
                                             vmem_limit_bytes=VMEM_LIMIT),
        name="mixer",
    )(x, *operands)


def _top16(vals, tie):
    width = vals.shape[1]
    slot = lax.broadcasted_iota(jnp.int32, (TOPK, width), 0)
    work = vals
    rank = jnp.full(vals.shape, NOT_SELECTED, F32)
    tops = jnp.zeros((TOPK, width), F32)
    for r in range(TOPK):
        m = jnp.max(work, axis=0, keepdims=True)
        first = jnp.min(jnp.where(work == m, tie, jnp.inf), axis=0, keepdims=True)
        hit = tie == first
        work = jnp.where(hit, -jnp.inf, work)
        rank = jnp.where(hit, float(r), rank)
        tops = jnp.where(slot == r, m, tops)
    return tops, rank


def _route_chunk(s1, s2):
    nk, width = s1.shape
    key_id = lax.broadcasted_iota(jnp.int32, (nk, width), 0).astype(F32)
    v1, r1 = _top16(s1, key_id)
    v2, r2 = _top16(s2, key_id)

    half = TOPK // 2
    pieces = [v1 + v2[j:j + 1, :] for j in range(half)] + [v1[0:1, :] + v2[half:TOPK, :]]
    cand = jnp.concatenate(pieces, axis=0)
    row = lax.broadcasted_iota(jnp.int32, cand.shape, 0)
    flat = jnp.where(row < half * TOPK, (row % TOPK) * TOPK + row // TOPK, row - half * TOPK + half)
    _, rank = _top16(cand, flat.astype(F32))
    sel = jnp.where(rank < float(TOPK), 1.0, 0.0)
    cnt = sel[0:TOPK, :]
    for j in range(1, half):
        cnt = cnt + sel[j * TOPK:(j + 1) * TOPK, :]
    tail = jnp.sum(sel[half * TOPK:, :], axis=0, keepdims=True)
    slot = lax.broadcasted_iota(jnp.int32, (TOPK, width), 0)
    cnt = jnp.where(slot == 0, cnt + tail, cnt)

    e1 = jnp.exp(v1 - v1[0:1, :])
    e2 = jnp.exp(v2 - v2[0:1, :])
    inner = jnp.zeros((TOPK, width), F32)
    for j in range(TOPK):
        inner = inner + jnp.where(cnt > float(j), e2[j:j + 1, :], 0.0)
    inv_z = 1.0 / jnp.sum(e1 * inner, axis=0, keepdims=True)

    cnt_key = jnp.zeros((nk, width), F32)
    for i in range(TOPK):
        cnt_key = cnt_key + jnp.where(r1 == float(i), cnt[i:i + 1, :], 0.0)
    f1 = jnp.exp(s1 - v1[0:1, :]) * inv_z
    f2 = jnp.exp(s2 - v2[0:1, :])
    return r2, f2, cnt_key, f1


def _oddeven_merge(lo, hi, r):
    step = r * 2
    if step < hi - lo:
        yield from _oddeven_merge(lo, hi, step)
        yield from _oddeven_merge(lo + r, hi, step)
        yield from [(i, i + r) for i in range(lo + r, hi - r, step)]
    else:
        yield (lo, lo + r)


def _oddeven_sort(lo, hi):
    if hi - lo >= 1:
        mid = lo + (hi - lo) // 2
        yield from _oddeven_sort(lo, mid)
        yield from _oddeven_sort(mid + 1, hi)
        yield from _oddeven_merge(lo, hi, 1)


SORT16 = tuple(_oddeven_sort(0, TOPK - 1))
BITONIC16 = tuple((i, i + d) for d in (8, 4, 2, 1) for i in range(TOPK) if (i & d) == 0)


def _exchange(a, pairs):
    for i, j in pairs:
        a[i], a[j] = jnp.maximum(a[i], a[j]), jnp.minimum(a[i], a[j])


def _over_sublanes(x, op):
    for shift in (4, 2, 1):
        x = op(x, pltpu.roll(x, shift, 0))
    return x


def _sorted_top16(regs):
    a = list(regs)
    _exchange(a, SORT16)
    for shift in (4, 2, 1):
        b = [pltpu.roll(x, shift, 0) for x in a]
        a = [jnp.maximum(a[v], b[TOPK - 1 - v]) for v in range(TOPK)]
        _exchange(a, BITONIC16)
    return a


def _search16(s, v):
    c1 = s < v[7]
    c2 = s < jnp.where(c1, v[11], v[3])
    c3 = s < jnp.where(c1, jnp.where(c2, v[13], v[9]), jnp.where(c2, v[5], v[1]))
    c4 = s < jnp.where(c1, jnp.where(c2, jnp.where(c3, v[14], v[12]), jnp.where(c3, v[10], v[8])),
                       jnp.where(c2, jnp.where(c3, v[6], v[4]), jnp.where(c3, v[2], v[0])))
    return s < v[15], c1, c2, c3, c4


def _mux16(c, vals):
    _, c1, c2, c3, c4 = c
    lvl = [jnp.where(c4, vals[2 * k + 1], vals[2 * k]) for k in range(8)]
    lvl = [jnp.where(c3, lvl[2 * k + 1], lvl[2 * k]) for k in range(4)]
    lvl = [jnp.where(c2, lvl[2 * k + 1], lvl[2 * k]) for k in range(2)]
    return jnp.where(c1, lvl[1], lvl[0])


def _route_chunk_fast(s1, s2):
    nk, width = s1.shape
    assert nk == TOPK * SUBLANES
    regs = lambda a: [a[r:r + SUBLANES, :] for r in range(0, nk, SUBLANES)]
    s1r, s2r = regs(s1), regs(s2)
    v1 = _sorted_top16(s1r)
    v2 = _sorted_top16(s2r)
    sub = lax.broadcasted_iota(jnp.int32, (SUBLANES, width), 0)
    one = lambda m: jnp.where(m, 1.0, 0.0)

    def column(vs):
        out = vs[0]
        for k in range(1, SUBLANES):
            out = jnp.where(sub == k, vs[k], out)
        return out

    v2lo, v2hi, v1hi = column(v2[:SUBLANES]), column(v2[SUBLANES:]), column(v1[SUBLANES:])
    cands = [v1[0] + v2lo, v1[0] + v2hi] + [v1[i] + v2lo for i in range(1, SUBLANES)] + [v1hi + v2[0]]
    work = list(cands)
    treemax = lambda ws: _over_sublanes(functools.reduce(jnp.maximum, ws), jnp.maximum)
    for _ in range(TOPK - 1):
        m = treemax(work)
        work = [jnp.where(w == m, -jnp.inf, w) for w in work]
    tau = treemax(work)
    keep = [c >= tau for c in cands]
    kf = [one(k) for k in keep]
    cnt = [_over_sublanes(kf[0] + kf[1], jnp.add)]
    cnt += [_over_sublanes(kf[i + 1], jnp.add) for i in range(1, SUBLANES)]
    cnt += [jnp.broadcast_to(kf[-1][k:k + 1, :], kf[-1].shape) for k in range(SUBLANES)]
    top = v1[0] + v2[0]
    z = _over_sublanes(functools.reduce(jnp.add, [jnp.where(k, jnp.exp(c - top), 0.0)
                                                  for k, c in zip(keep, cands)]), jnp.add)
    inv_z = 1.0 / z

    bad = one(functools.reduce(jnp.add, cnt) != float(TOPK))
    for v in (v1, v2):
        for r in range(TOPK - 1):
            bad = jnp.maximum(bad, one(v[r] == v[r + 1]))

    r2, f2, cnt_key, f1 = [], [], [], []
    above1 = jnp.zeros((SUBLANES, width), F32)
    above2 = jnp.zeros((SUBLANES, width), F32)
    for s, t in zip(s1r, s2r):
        c = _search16(s, v1)
        cnt_key.append(jnp.where(c[0], 0.0, _mux16(c, cnt)))
        f1.append(jnp.exp(s - v1[0]) * inv_z)
        above1 = above1 + jnp.where(c[0], 0.0, 1.0)
        c = _search16(t, v2)
        rank = (jnp.where(c[1], 8.0, 0.0) + jnp.where(c[2], 4.0, 0.0)) + (jnp.where(c[3], 2.0, 0.0)
                                                                          + jnp.where(c[4], 1.0, 0.0))
        r2.append(jnp.where(c[0], NOT_SELECTED, rank))
        f2.append(jnp.exp(t - v2[0]))
        above2 = above2 + jnp.where(c[0], 0.0, 1.0)
    for above in (above1, above2):
        bad = jnp.maximum(bad, one(_over_sublanes(above, jnp.add) != float(TOPK)))
    cat = lambda rs: jnp.concatenate(rs, axis=0)
    return cat(r2), cat(f2), cat(cnt_key), cat(f1), bad


def _route_kernel(x2_ref, g2_ref, wqt_ref, k1_ref, k2_ref,
                  xnt_ref, r2_ref, f2_ref, cnt_ref, f1_ref, qt_ref):
    head = pl.program_id(1)
    nk = k1_ref.shape[0]
    half = k1_ref.shape[1]
    tb = x2_ref.shape[0]

    @pl.when(head == 0)
    def _():
        x = x2_ref[...]
        xn = x * _rms_scale(x) * g2_ref[...]
        xnt = xn.T.astype(BF16)
        xnt_ref[...] = xnt
        qt_ref[...] = jnp.dot(wqt_ref[...], xnt, preferred_element_type=F32).astype(BF16)

    base = pl.multiple_of(head * (2 * half), 2 * half)
    s1 = jnp.dot(k1_ref[...], qt_ref[pl.ds(base, half), :], preferred_element_type=F32)
    s2 = jnp.dot(k2_ref[...], qt_ref[pl.ds(base + half, half), :], preferred_element_type=F32)
    def store(cols, r2, f2, cnt_key, f1):
        r2_ref[:, cols] = r2.astype(BF16)
        f2_ref[:, cols] = f2.astype(BF16)
        cnt_ref[:, cols] = cnt_key
        f1_ref[:, cols] = f1

    chunks = [slice(c * LANES, (c + 1) * LANES) for c in range(tb // LANES)]
    bad = jnp.zeros((SUBLANES, LANES), F32)
    for cols in chunks:
        *tables, tie = _route_chunk_fast(s1[:, cols], s2[:, cols])
        store(cols, *tables)
        bad = jnp.maximum(bad, tie)

    @pl.when(jnp.max(bad) > 0.0)
    def _():
        for cols in chunks:
            store(cols, *_route_chunk(s1[:, cols], s2[:, cols]))
    del nk


def _route(x2, norm2_g, wq_t, k1, k2, *, tb, heads):
    n, d = x2.shape
    nk, half = k1.shape
    tok = lambda dt: jax.ShapeDtypeStruct((heads, nk, n), dt)
    per_head = pl.BlockSpec((None, nk, tb), lambda i, h: (h, 0, i))
    const = lambda a: pl.BlockSpec(a.shape, lambda i, h: (0,) * a.ndim)
    g2 = norm2_g.reshape(1, d).astype(F32)
    return pl.pallas_call(
        _route_kernel,
        out_shape=(jax.ShapeDtypeStruct((d, n), BF16), tok(BF16), tok(BF16), tok(F32), tok(F32)),
        grid=(n // tb, heads),
        in_specs=[pl.BlockSpec((tb, d), lambda i, h: (i, 0)), const(g2), const(wq_t), const(k1), const(k2)],
        out_specs=(pl.BlockSpec((d, tb), lambda i, h: (0, i)), per_head, per_head, per_head, per_head),
        scratch_shapes=[pltpu.VMEM((wq_t.shape[0], tb), BF16)],
        compiler_params=pltpu.CompilerParams(dimension_semantics=("arbitrary", "arbitrary"),
                                             vmem_limit_bytes=VMEM_LIMIT),
        name="route",
    )(x2, g2, wq_t, k1, k2)


def _peer_kernel(xnt_ref, u_ref, vt_ref, r2_ref, f2_ref, cnt_ref, f1_ref, x2_ref, gf_ref, o_ref,
                 acc_ref, at_ref, wt_ref):
    j = pl.program_id(1)
    heads, nk, tm = r2_ref.shape
    eb = u_ref.shape[0]
    blocks = eb // nk

    @pl.when(j == 0)
    def _():
        acc_ref[...] = jnp.zeros(acc_ref.shape, F32)

    at_ref[...] = jnp.dot(u_ref[...], xnt_ref[...], preferred_element_type=F32)
    for b in range(blocks):
        e1 = j * blocks + b
        rows = slice(b * nk, (b + 1) * nk)
        act = _gelu(at_ref[rows, :]).astype(BF16)
        gate = jnp.zeros((nk, tm), BF16)
        for h in range(heads):
            cnt = cnt_ref[h, pl.ds(e1, 1), :].astype(BF16)
            f1 = f1_ref[h, pl.ds(e1, 1), :].astype(BF16)
            gate = gate + jnp.where(r2_ref[h] < cnt, f2_ref[h], jnp.zeros((), BF16)) * f1
        wt_ref[rows, :] = act * gate
    acc_ref[...] += jnp.dot(vt_ref[...], wt_ref[...], preferred_element_type=F32)

    @pl.when(j == pl.num_programs(1) - 1)
    def _():
        y = x2_ref[...] + acc_ref[...].T
        o_ref[...] = y * _rms_scale(y) * gf_ref[...]


def _peer(xnt, u_bf, vt_bf, r2, f2, cnt, f1, x2, final_g, *, tm, eb):
    d, n = xnt.shape
    ne = u_bf.shape[0]
    heads, nk, _ = r2.shape
    gf = final_g.reshape(1, d).astype(F32)
    tables = pl.BlockSpec((heads, nk, tm), lambda i, j: (0, 0, i))
    return pl.pallas_call(
        _peer_kernel,
        out_shape=jax.ShapeDtypeStruct((n, d), F32),
        grid=(n // tm, ne // eb),
        in_specs=[pl.BlockSpec((d, tm), lambda i, j: (0, i)),
                  pl.BlockSpec((eb, d), lambda i, j: (j, 0)),
                  pl.BlockSpec((d, eb), lambda i, j: (0, j)),
                  tables, tables, tables, tables,
                  pl.BlockSpec((tm, d), lambda i, j: (i, 0)),
                  pl.BlockSpec((1, d), lambda i, j: (0, 0))],
        out_specs=pl.BlockSpec((tm, d), lambda i, j: (i, 0)),
        scratch_shapes=[pltpu.VMEM((d, tm), F32), pltpu.VMEM((eb, tm), F32), pltpu.VMEM((eb, tm), BF16)],
        compiler_params=pltpu.CompilerParams(dimension_semantics=("arbitrary", "arbitrary"),
                                             vmem_limit_bytes=VMEM_LIMIT),
        name="peer",
    )(xnt, u_bf, vt_bf, r2, f2, cnt, f1, x2, gf)


def _layer(x, norm1_g, w_in, lam_re, lam_im, log_dt, b_re, b_im, c_re, c_im, ssm_d, glu_w, glu_b,
           conv_w, conv_b, ln_g, ln_b, w_out, norm2_g, wq, k1, k2, u_tab, v_tab, out_g,
           *, tl, tb, tm, eb):
    bsz, seq, d = x.shape
    heads = wq.shape[1] // (2 * k1.shape[1])
    tables = _s5_tables(lam_re, lam_im, log_dt, b_re, b_im, c_re, c_im, glu_w)
    x2 = _mixer(x, norm1_g, w_in, tables, ssm_d, glu_b, conv_w, conv_b, ln_g, ln_b, w_out, tl=tl)
    x2 = x2.reshape(bsz * seq, d)
    xnt, r2, f2, cnt, f1 = _route(x2, norm2_g, wq.T.astype(BF16), k1.astype(BF16), k2.astype(BF16),
                                  tb=tb, heads=heads)
    out = _peer(xnt, u_tab.astype(BF16), v_tab.T.astype(BF16), r2, f2, cnt, f1, x2, out_g, tm=tm, eb=eb)
    return out.reshape(bsz, seq, d)


def kernel(x, norm1_g, w_in, lam_re, lam_im, log_dt, ssm_b_re, ssm_b_im, ssm_c_re, ssm_c_im, ssm_d, glu_w,
           glu_b, conv_w, conv_b, conv_ln_g, conv_ln_b, w_out, norm2_g, peer_wq, peer_k1, peer_k2, peer_u,
           peer_v, final_g):
    depth = w_in.shape[0]
    d = x.shape[-1]
    for l in range(depth):
        last = l == depth - 1
        assert last, "only the single-layer configuration is implemented"
        x = _layer(x, norm1_g[l], w_in[l], lam_re[l], lam_im[l], log_dt[l], ssm_b_re[l], ssm_b_im[l],
                   ssm_c_re[l], ssm_c_im[l], ssm_d[l], glu_w[l], glu_b[l], conv_w[l], conv_b[l],
                   conv_ln_g[l], conv_ln_b[l], w_out[l], norm2_g[l], peer_wq[l], peer_k1[l], peer_k2[l],
                   peer_u[l], peer_v[l], final_g, tl=256, tb=256, tm=512, eb=1024)
    del d
    return x
```

```python
import functools
import math

import jax
import jax.numpy as jnp
from jax import lax
from jax.experimental import pallas as pl
from jax.experimental.pallas import tpu as pltpu

F32 = jnp.float32
BF16 = jnp.bfloat16

EPS = 1e-6
TOPK = 16
SUBLANES = 8
LANES = 128
VMEM_LIMIT = 56 * 1024 * 1024
NOT_SELECTED = 127.0


def _gelu(x):
    return 0.5 * x * (1.0 + lax.erf(x * (1.0 / math.sqrt(2.0))))


def _sigmoid(x):
    return 1.0 / (1.0 + jnp.exp(-x))


def _rms_scale(x):
    return lax.rsqrt(jnp.mean(x * x, axis=-1, keepdims=True) + EPS)


SCAN_LANES = 256
CONV_ROWS = 32
HIST = 32


def _mixer_kernel(x_ref, g1_ref, win_ref, bbd_ref, cbd_ref, m_ref, ap_ref, d_ref, glu_ref, glub_ref,
                  cw_ref, cb_ref, lng_ref, lnb_ref, wout_ref, o_ref,
                  bu_ref, carry_ref, hbuf_ref, ycat_ref):
    tl = x_ref.shape[0]
    ns = carry_ref.shape[-1]
    ds = d_ref.shape[-1]
    dc = cb_ref.shape[-1]
    width = cw_ref.shape[0]

    @pl.when(pl.program_id(1) == 0)
    def _():
        carry_ref[...] = jnp.zeros(carry_ref.shape, F32)
        hbuf_ref[0:HIST, :] = jnp.zeros((HIST, dc), F32)

    x = x_ref[...]
    h = x * _rms_scale(x) * g1_ref[...]
    z = jnp.dot(h.astype(BF16), win_ref[...], preferred_element_type=F32)
    u = z[:, :ds]
    v = z[:, ds:ds + dc]
    g = z[:, ds + dc:]

    bu_ref[...] = jnp.dot(u.astype(BF16), bbd_ref[...], preferred_element_type=F32)

    for c in range(ns // SCAN_LANES):
        lo = c * SCAN_LANES
        re = slice(lo, lo + SCAN_LANES)
        im = slice(ns + lo, ns + lo + SCAN_LANES)
        levels = [(m_ref[2 * i, :, re], m_ref[2 * i + 1, :, re], 1 << i) for i in range(3)]
        pr = ap_ref[0, :, re]
        pi = ap_ref[1, :, re]

        def body(r, carry, re=re, im=im, levels=levels, pr=pr, pi=pi):
            cr, ci = carry
            rows = pl.ds(pl.multiple_of(r * SUBLANES, SUBLANES), SUBLANES)
            xr = bu_ref[rows, re]
            xi = bu_ref[rows, im]
            for mr, mi, shift in levels:
                sr = pltpu.roll(xr, shift, 0)
                si = pltpu.roll(xi, shift, 0)
                xr, xi = xr + mr * sr - mi * si, xi + mr * si + mi * sr
            xr, xi = xr + pr * cr - pi * ci, xi + pr * ci + pi * cr
            bu_ref[rows, re] = xr
            bu_ref[rows, im] = xi
            last = SUBLANES - 1
            return (jnp.broadcast_to(xr[last:last + 1, :], xr.shape),
                    jnp.broadcast_to(xi[last:last + 1, :], xi.shape))

        cr, ci = lax.fori_loop(0, tl // SUBLANES, body, (carry_ref[0, :, re], carry_ref[1, :, re]))
        carry_ref[0, :, re] = cr
        carry_ref[1, :, re] = ci

    y = jnp.dot(bu_ref[...].astype(BF16), cbd_ref[...], preferred_element_type=F32) + d_ref[...] * u
    yg = _gelu(y)
    gate = jnp.dot(yg.astype(BF16), glu_ref[...], preferred_element_type=F32) + glub_ref[...]
    ycat_ref[:, 0:ds] = (yg * _sigmoid(gate)).astype(BF16)

    hbuf_ref[HIST:HIST + tl, :] = v * _sigmoid(g)
    for r0 in range(0, tl, CONV_ROWS):
        acc = jnp.broadcast_to(cb_ref[...], (CONV_ROWS, dc))
        for k in range(width):
            start = r0 + HIST - (width - 1) + k
            acc = acc + cw_ref[k:k + 1, :] * hbuf_ref[start:start + CONV_ROWS, :]
        mu = jnp.mean(acc, axis=-1, keepdims=True)
        cen = acc - mu
        var = jnp.mean(cen * cen, axis=-1, keepdims=True)
        hn = cen * lax.rsqrt(var + EPS) * lng_ref[...] + lnb_ref[...]
        ycat_ref[r0:r0 + CONV_ROWS, ds:ds + dc] = (hn * _sigmoid(hn)).astype(BF16)
    hbuf_ref[0:HIST, :] = hbuf_ref[tl:tl + HIST, :]

    o_ref[...] = x + jnp.dot(ycat_ref[...], wout_ref[...], preferred_element_type=F32)


def _s5_tables(lam_re, lam_im, log_dt, b_re, b_im, c_re, c_im, glu_w):
    ng, ns = lam_re.shape
    gh = b_re.shape[-1]
    dt = jnp.exp(log_dt)[:, None]
    mag = jnp.exp(lam_re * dt)
    ang = lam_im * dt
    a_re = mag * jnp.cos(ang)
    a_im = mag * jnp.sin(ang)
    num_re = a_re - 1.0
    num_im = a_im
    den = lam_re * lam_re + lam_im * lam_im
    f_re = (num_re * lam_re + num_im * lam_im) / den
    f_im = (num_im * lam_re - num_re * lam_im) / den
    bb_re = f_re[..., None] * b_re - f_im[..., None] * b_im
    bb_im = f_re[..., None] * b_im + f_im[..., None] * b_re
    eye = jnp.eye(ng, dtype=F32)
    to_state = lambda m: jnp.einsum('gph,gk->ghkp', m, eye).reshape(ng * gh, ng * ns)
    from_state = lambda m: jnp.einsum('ghp,gk->gpkh', m, eye).reshape(ng * ns, ng * gh)
    bbd = jnp.concatenate([to_state(bb_re), to_state(bb_im)], axis=1).astype(BF16)
    cbd = jnp.concatenate([from_state(c_re), -from_state(c_im)], axis=0).astype(BF16)
    glu_bd = jnp.einsum('ghk,gm->ghmk', glu_w, eye).reshape(ng * gh, ng * gh).astype(BF16)

    ar = a_re.reshape(1, ng * ns)
    ai = a_im.reshape(1, ng * ns)
    powers = [(ar, ai)]
    for _ in range(SUBLANES - 1):
        qr, qi = powers[-1]
        powers.append((qr * ar - qi * ai, qr * ai + qi * ar))
    row = jnp.arange(SUBLANES)[:, None]
    shift_mult = []
    for shift in (1, 2, 4):
        qr, qi = powers[shift - 1]
        shift_mult.append(jnp.where(row >= shift, qr, 0.0))
        shift_mult.append(jnp.where(row >= shift, qi, 0.0))
    shift_mult = jnp.stack(shift_mult)
    carry_mult = jnp.stack([jnp.concatenate([p[0] for p in powers], axis=0),
                            jnp.concatenate([p[1] for p in powers], axis=0)])
    return bbd, cbd, glu_bd, shift_mult, carry_mult


def _mixer(x, norm1_g, w_in, tables, ssm_d, glu_b, conv_w, conv_b, ln_g, ln_b, w_out, *, tl):
    bsz, seq, d = x.shape
    bbd, cbd, glu_bd, shift_mult, carry_mult = tables
    ds = ssm_d.shape[-1]
    dc = conv_b.shape[-1]
    ns = carry_mult.shape[-1]
    row = lambda a: a.reshape(1, -1).astype(F32)
    const = lambda a: pl.BlockSpec(a.shape, lambda b, t: (0,) * a.ndim)
    operands = [row(norm1_g), w_in.astype(BF16), bbd, cbd, shift_mult, carry_mult, row(ssm_d), glu_bd,
                row(glu_b), conv_w.astype(F32), row(conv_b), row(ln_g), row(ln_b), w_out.astype(BF16)]
    return pl.pallas_call(
        _mixer_kernel,
        out_shape=jax.ShapeDtypeStruct((bsz, seq, d), F32),
        grid=(bsz, seq // tl),
        in_specs=[pl.BlockSpec((None, tl, d), lambda b, t: (b, t, 0))] + [const(a) for a in operands],
        out_specs=pl.BlockSpec((None, tl, d), lambda b, t: (b, t, 0)),
        scratch_shapes=[pltpu.VMEM((tl, 2 * ns), F32),
                        pltpu.VMEM((2, SUBLANES, ns), F32),
                        pltpu.VMEM((tl + HIST, dc), F32),
                        pltpu.VMEM((tl, ds + dc), BF16)],
        compiler_params=pltpu.CompilerParams(dimension_semantics=("arbitrary", "arbitrary"),
                                             vmem_limit_bytes=VMEM_LIMIT),
        name="mixer",
    )(x, *operands)


def _top16(vals, tie):
    width = vals.shape[1]
    slot = lax.broadcasted_iota(jnp.int32, (TOPK, width), 0)
    work = vals
    rank = jnp.full(vals.shape, NOT_SELECTED, F32)
    tops = jnp.zeros((TOPK, width), F32)
    for r in range(TOPK):
        m = jnp.max(work, axis=0, keepdims=True)
        first = jnp.min(jnp.where(work == m, tie, jnp.inf), axis=0, keepdims=True)
        hit = tie == first
        work = jnp.where(hit, -jnp.inf, work)
        rank = jnp.where(hit, float(r), rank)
        tops = jnp.where(slot == r, m, tops)
    return tops, rank


def _route_chunk(s1, s2):
    nk, width = s1.shape
    key_id = lax.broadcasted_iota(jnp.int32, (nk, width), 0).astype(F32)
    v1, r1 = _top16(s1, key_id)
    v2, r2 = _top16(s2, key_id)

    half = TOPK // 2
    pieces = [v1 + v2[j:j + 1, :] for j in range(half)] + [v1[0:1, :] + v2[half:TOPK, :]]
    cand = jnp.concatenate(pieces, axis=0)
    row = lax.broadcasted_iota(jnp.int32, cand.shape, 0)
    flat = jnp.where(row < half * TOPK, (row % TOPK) * TOPK + row // TOPK, row - half * TOPK + half)
    _, rank = _top16(cand, flat.astype(F32))
    sel = jnp.where(rank < float(TOPK), 1.0, 0.0)
    cnt = sel[0:TOPK, :]
    for j in range(1, half):
        cnt = cnt + sel[j * TOPK:(j + 1) * TOPK, :]
    tail = jnp.sum(sel[half * TOPK:, :], axis=0, keepdims=True)
    slot = lax.broadcasted_iota(jnp.int32, (TOPK, width), 0)
    cnt = jnp.where(slot == 0, cnt + tail, cnt)

    e1 = jnp.exp(v1 - v1[0:1, :])
    e2 = jnp.exp(v2 - v2[0:1, :])
    inner = jnp.zeros((TOPK, width), F32)
    for j in range(TOPK):
        inner = inner + jnp.where(cnt > float(j), e2[j:j + 1, :], 0.0)
    inv_z = 1.0 / jnp.sum(e1 * inner, axis=0, keepdims=True)

    cnt_key = jnp.zeros((nk, width), F32)
    for i in range(TOPK):
        cnt_key = cnt_key + jnp.where(r1 == float(i), cnt[i:i + 1, :], 0.0)
    f1 = jnp.exp(s1 - v1[0:1, :]) * inv_z
    f2 = jnp.exp(s2 - v2[0:1, :])
    return r2, f2, cnt_key, f1


def _oddeven_merge(lo, hi, r):
    step = r * 2
    if step < hi - lo:
        yield from _oddeven_merge(lo, hi, step)
        yield from _oddeven_merge(lo + r, hi, step)
        yield from [(i, i + r) for i in range(lo + r, hi - r, step)]
    else:
        yield (lo, lo + r)


def _oddeven_sort(lo, hi):
    if hi - lo >= 1:
        mid = lo + (hi - lo) // 2
        yield from _oddeven_sort(lo, mid)
        yield from _oddeven_sort(mid + 1, hi)
        yield from _oddeven_merge(lo, hi, 1)


SORT16 = tuple(_oddeven_sort(0, TOPK - 1))
BITONIC16 = tuple((i, i + d) for d in (8, 4, 2, 1) for i in range(TOPK) if (i & d) == 0)


def _exchange(a, pairs):
    for i, j in pairs:
        a[i], a[j] = jnp.maximum(a[i], a[j]), jnp.minimum(a[i], a[j])


def _over_sublanes(x, op):
    for shift in (4, 2, 1):
        x = op(x, pltpu.roll(x, shift, 0))
    return x


def _sorted_top16(regs):
    a = list(regs)
    _exchange(a, SORT16)
    for shift in (4, 2, 1):
        b = [pltpu.roll(x, shift, 0) for x in a]
        a = [jnp.maximum(a[v], b[TOPK - 1 - v]) for v in range(TOPK)]
        _exchange(a, BITONIC16)
    return a


def _search16(s, v):
    c1 = s < v[7]
    c2 = s < jnp.where(c1, v[11], v[3])
    c3 = s < jnp.where(c1, jnp.where(c2, v[13], v[9]), jnp.where(c2, v[5], v[1]))
    c4 = s < jnp.where(c1, jnp.where(c2, jnp.where(c3, v[14], v[12]), jnp.where(c3, v[10], v[8])),
                       jnp.where(c2, jnp.where(c3, v[6], v[4]), jnp.where(c3, v[2], v[0])))
    return s < v[15], c1, c2, c3, c4


def _mux16(c, vals):
    _, c1, c2, c3, c4 = c
    lvl = [jnp.where(c4, vals[2 * k + 1], vals[2 * k]) for k in range(8)]
    lvl = [jnp.where(c3, lvl[2 * k + 1], lvl[2 * k]) for k in range(4)]
    lvl = [jnp.where(c2, lvl[2 * k + 1], lvl[2 * k]) for k in range(2)]
    return jnp.where(c1, lvl[1], lvl[0])


def _route_chunk_fast(s1, s2):
    nk, width = s1.shape
    assert nk == TOPK * SUBLANES
    regs = lambda a: [a[r:r + SUBLANES, :] for r in range(0, nk, SUBLANES)]
    s1r, s2r = regs(s1), regs(s2)
    v1 = _sorted_top16(s1r)
    v2 = _sorted_top16(s2r)
    sub = lax.broadcasted_iota(jnp.int32, (SUBLANES, width), 0)
    one = lambda m: jnp.where(m, 1.0, 0.0)

    def column(vs):
        out = vs[0]
        for k in range(1, SUBLANES):
            out = jnp.where(sub == k, vs[k], out)
        return out

    v2lo, v2hi, v1hi = column(v2[:SUBLANES]), column(v2[SUBLANES:]), column(v1[SUBLANES:])
    cands = [v1[0] + v2lo, v1[0] + v2hi] + [v1[i] + v2lo for i in range(1, SUBLANES)] + [v1hi + v2[0]]
    work = list(cands)
    treemax = lambda ws: _over_sublanes(functools.reduce(jnp.maximum, ws), jnp.maximum)
    for _ in range(TOPK - 1):
        m = treemax(work)
        work = [jnp.where(w == m, -jnp.inf, w) for w in work]
    tau = treemax(work)
    keep = [c >= tau for c in cands]
    kf = [one(k) for k in keep]
    cnt = [_over_sublanes(kf[0] + kf[1], jnp.add)]
    cnt += [_over_sublanes(kf[i + 1], jnp.add) for i in range(1, SUBLANES)]
    cnt += [jnp.broadcast_to(kf[-1][k:k + 1, :], kf[-1].shape) for k in range(SUBLANES)]
    top = v1[0] + v2[0]
    z = _over_sublanes(functools.reduce(jnp.add, [jnp.where(k, jnp.exp(c - top), 0.0)
                                                  for k, c in zip(keep, cands)]), jnp.add)
    inv_z = 1.0 / z

    bad = one(functools.reduce(jnp.add, cnt) != float(TOPK))
    for v in (v1, v2):
        for r in range(TOPK - 1):
            bad = jnp.maximum(bad, one(v[r] == v[r + 1]))

    r2, f2, cnt_key, f1 = [], [], [], []
    above1 = jnp.zeros((SUBLANES, width), F32)
    above2 = jnp.zeros((SUBLANES, width), F32)
    for s, t in zip(s1r, s2r):
        c = _search16(s, v1)
        cnt_key.append(jnp.where(c[0], 0.0, _mux16(c, cnt)))
        f1.append(jnp.exp(s - v1[0]) * inv_z)
        above1 = above1 + jnp.where(c[0], 0.0, 1.0)
        c = _search16(t, v2)
        rank = (jnp.where(c[1], 8.0, 0.0) + jnp.where(c[2], 4.0, 0.0)) + (jnp.where(c[3], 2.0, 0.0)
                                                                          + jnp.where(c[4], 1.0, 0.0))
        r2.append(jnp.where(c[0], NOT_SELECTED, rank))
        f2.append(jnp.exp(t - v2[0]))
        above2 = above2 + jnp.where(c[0], 0.0, 1.0)
    for above in (above1, above2):
        bad = jnp.maximum(bad, one(_over_sublanes(above, jnp.add) != float(TOPK)))
    cat = lambda rs: jnp.concatenate(rs, axis=0)
    return cat(r2), cat(f2), cat(cnt_key), cat(f1), bad


def _route_kernel(x2_ref, g2_ref, wqt_ref, k1_ref, k2_ref,
                  xnt_ref, r2_ref, f2_ref, cnt_ref, f1_ref, qt_ref):
    head = pl.program_id(1)
    nk = k1_ref.shape[0]
    half = k1_ref.shape[1]
    tb = x2_ref.shape[0]

    @pl.when(head == 0)
    def _():
        x = x2_ref[...]
        xn = x * _rms_scale(x) * g2_ref[...]
        xnt = xn.T.astype(BF16)
        xnt_ref[...] = xnt
        qt_ref[...] = jnp.dot(wqt_ref[...], xnt, preferred_element_type=F32).astype(BF16)

    base = pl.multiple_of(head * (2 * half), 2 * half)
    s1 = jnp.dot(k1_ref[...], qt_ref[pl.ds(base, half), :], preferred_element_type=F32)
    s2 = jnp.dot(k2_ref[...], qt_ref[pl.ds(base + half, half), :], preferred_element_type=F32)
    def store(cols, r2, f2, cnt_key, f1):
        r2_ref[:, cols] = r2.astype(BF16)
        f2_ref[:, cols] = f2.astype(BF16)
        cnt_ref[:, cols] = cnt_key
        f1_ref[:, cols] = f1

    chunks = [slice(c * LANES, (c + 1) * LANES) for c in range(tb // LANES)]
    bad = jnp.zeros((SUBLANES, LANES), F32)
    for cols in chunks:
        *tables, tie = _route_chunk_fast(s1[:, cols], s2[:, cols])
        store(cols, *tables)
        bad = jnp.maximum(bad, tie)

    @pl.when(jnp.max(bad) > 0.0)
    def _():
        for cols in chunks:
            store(cols, *_route_chunk(s1[:, cols], s2[:, cols]))
    del nk


def _route(x2, norm2_g, wq_t, k1, k2, *, tb, heads):
    n, d = x2.shape
    nk, half = k1.shape
    tok = lambda dt: jax.ShapeDtypeStruct((heads, nk, n), dt)
    per_head = pl.BlockSpec((None, nk, tb), lambda i, h: (h, 0, i))
    const = lambda a: pl.BlockSpec(a.shape, lambda i, h: (0,) * a.ndim)
    g2 = norm2_g.reshape(1, d).astype(F32)
    return pl.pallas_call(
        _route_kernel,
        out_shape=(jax.ShapeDtypeStruct((d, n), BF16), tok(BF16), tok(BF16), tok(F32), tok(F32)),
        grid=(n // tb, heads),
        in_specs=[pl.BlockSpec((tb, d), lambda i, h: (i, 0)), const(g2), const(wq_t), const(k1), const(k2)],
        out_specs=(pl.BlockSpec((d, tb), lambda i, h: (0, i)), per_head, per_head, per_head, per_head),
        scratch_shapes=[pltpu.VMEM((wq_t.shape[0], tb), BF16)],
        compiler_params=pltpu.CompilerParams(dimension_semantics=("arbitrary", "arbitrary"),
                                             vmem_limit_bytes=VMEM_LIMIT),
        name="route",
    )(x2, g2, wq_t, k1, k2)


def _peer_kernel(xnt_ref, u_ref, vt_ref, r2_ref, f2_ref, cnt_ref, f1_ref, x2_ref, gf_ref, o_ref,
                 acc_ref, at_ref, wt_ref):
    j = pl.program_id(1)
    heads, nk, tm = r2_ref.shape
    eb = u_ref.shape[0]
    blocks = eb // nk

    last = pl.num_programs(1) - 1
    cur = j % 2

    @pl.when((pl.program_id(0) == 0) & (j == 0))
    def _():
        acc_ref[...] = jnp.zeros(acc_ref.shape, F32)
        wt_ref[...] = jnp.zeros(wt_ref.shape, BF16)

    @pl.when(j < last)
    def _():
        applied = acc_ref[...] + jnp.dot(vt_ref[...], wt_ref[1 - cur], preferred_element_type=F32)
        acc_ref[...] = jnp.where(j > 0, applied, 0.0)
        at_ref[...] = jnp.dot(u_ref[...], xnt_ref[...], preferred_element_type=F32)
        for b in range(blocks):
            e1 = j * blocks + b
            rows = slice(b * nk, (b + 1) * nk)
            act = _gelu(at_ref[rows, :]).astype(BF16)
            gate = jnp.zeros((nk, tm), BF16)
            for h in range(heads):
                cnt = cnt_ref[h, pl.ds(e1, 1), :].astype(BF16)
                f1 = f1_ref[h, pl.ds(e1, 1), :].astype(BF16)
                gate = gate + jnp.where(r2_ref[h] < cnt, f2_ref[h], jnp.zeros((), BF16)) * f1
            wt_ref[cur, rows, :] = act * gate

    @pl.when(j == last)
    def _():
        acc = acc_ref[...] + jnp.dot(vt_ref[...], wt_ref[1 - cur], preferred_element_type=F32)
        y = x2_ref[...] + acc.T
        o_ref[...] = y * _rms_scale(y) * gf_ref[...]


def _peer(xnt, u_bf, vt_bf, r2, f2, cnt, f1, x2, final_g, *, tm, eb):
    d, n = xnt.shape
    ne = u_bf.shape[0]
    heads, nk, _ = r2.shape
    gf = final_g.reshape(1, d).astype(F32)
    nblk = ne // eb
    tables = pl.BlockSpec((heads, nk, tm), lambda i, j: (0, 0, i))
    return pl.pallas_call(
        _peer_kernel,
        out_shape=jax.ShapeDtypeStruct((n, d), F32),
        grid=(n // tm, nblk + 1),
        in_specs=[pl.BlockSpec((d, tm), lambda i, j: (0, i)),
                  pl.BlockSpec((eb, d), lambda i, j: (jnp.minimum(j, nblk - 1), 0)),
                  pl.BlockSpec((d, eb), lambda i, j: (0, jnp.maximum(j - 1, 0))),
                  tables, tables, tables, tables,
                  pl.BlockSpec((tm, d), lambda i, j: (i, 0)),
                  pl.BlockSpec((1, d), lambda i, j: (0, 0))],
        out_specs=pl.BlockSpec((tm, d), lambda i, j: (i, 0)),
        scratch_shapes=[pltpu.VMEM((d, tm), F32), pltpu.VMEM((eb, tm), F32), pltpu.VMEM((2, eb, tm), BF16)],
        compiler_params=pltpu.CompilerParams(dimension_semantics=("arbitrary", "arbitrary"),
                                             vmem_limit_bytes=VMEM_LIMIT),
        name="peer",
    )(xnt, u_bf, vt_bf, r2, f2, cnt, f1, x2, gf)


def _layer(x, norm1_g, w_in, lam_re, lam_im, log_dt, b_re, b_im, c_re, c_im, ssm_d, glu_w, glu_b,
           conv_w, conv_b, ln_g, ln_b, w_out, norm2_g, wq, k1, k2, u_tab, v_tab, out_g,
           *, tl, tb, tm, eb):
    bsz, seq, d = x.shape
    heads = wq.shape[1] // (2 * k1.shape[1])
    tables = _s5_tables(lam_re, lam_im, log_dt, b_re, b_im, c_re, c_im, glu_w)
    x2 = _mixer(x, norm1_g, w_in, tables, ssm_d, glu_b, conv_w, conv_b, ln_g, ln_b, w_out, tl=tl)
    x2 = x2.reshape(bsz * seq, d)
    xnt, r2, f2, cnt, f1 = _route(x2, norm2_g, wq.T.astype(BF16), k1.astype(BF16), k2.astype(BF16),
                                  tb=tb, heads=heads)
    out = _peer(xnt, u_tab.astype(BF16), v_tab.T.astype(BF16), r2, f2, cnt, f1, x2, out_g, tm=tm, eb=eb)
    return out.reshape(bsz, seq, d)


def kernel(x, norm1_g, w_in, lam_re, lam_im, log_dt, ssm_b_re, ssm_b_im, ssm_c_re, ssm_c_im, ssm_d, glu_w,
           glu_b, conv_w, conv_b, conv_ln_g, conv_ln_b, w_out, norm2_g, peer_wq, peer_k1, peer_k2, peer_u,
           peer_v, final_g):
    depth = w_in.shape[0]
    d = x.shape[-1]
    for l in range(depth):
        last = l == depth - 1
        assert last, "only the single-layer configuration is implemented"
        x = _layer(x, norm1_g[l], w_in[l], lam_re[l], lam_im[l], log_dt[l], ssm_b_re[l], ssm_b_im[l],
                   ssm_c_re[l], ssm_c_im[l], ssm_d[l], glu_w[l], glu_b[l], conv_w[l], conv_b[l],
                   conv_ln_g[l], conv_ln_b[l], w_out[l], norm2_g[l], peer_wq[l], peer_k1[l], peer_k2[l],
                   peer_u[l], peer_v[l], final_g, tl=256, tb=256, tm=512, eb=1024)
    del d
    return x
```

```python
import functools
import math

import jax
import jax.numpy as jnp
from jax import lax
from jax.experimental import pallas as pl
from jax.experimental.pallas import tpu as pltpu

F32 = jnp.float32
BF16 = jnp.bfloat16

EPS = 1e-6
TOPK = 16
SUBLANES = 8
LANES = 128
VMEM_LIMIT = 56 * 1024 * 1024
NOT_SELECTED = 127.0


def _gelu(x):
    return 0.5 * x * (1.0 + lax.erf(x * (1.0 / math.sqrt(2.0))))


def _sigmoid(x):
    return 1.0 / (1.0 + jnp.exp(-x))


def _rms_scale(x):
    return lax.rsqrt(jnp.mean(x * x, axis=-1, keepdims=True) + EPS)


SCAN_LANES = 256
CONV_ROWS = 32
HIST = 32


def _mixer_kernel(x_ref, g1_ref, win_ref, bbd_ref, cbd_ref, m_ref, ap_ref, d_ref, glu_ref, glub_ref,
                  cw_ref, cb_ref, lng_ref, lnb_ref, wout_ref, o_ref,
                  bu_ref, carry_ref, hbuf_ref, ycat_ref):
    tl = x_ref.shape[0]
    ns = carry_ref.shape[-1]
    ds = d_ref.shape[-1]
    dc = cb_ref.shape[-1]
    width = cw_ref.shape[0]

    @pl.when(pl.program_id(1) == 0)
    def _():
        carry_ref[...] = jnp.zeros(carry_ref.shape, F32)
        hbuf_ref[0:HIST, :] = jnp.zeros((HIST, dc), F32)
        hbuf_ref[HIST + tl:HIST + tl + SUBLANES, :] = jnp.zeros((SUBLANES, dc), F32)

    x = x_ref[...]
    h = x * _rms_scale(x) * g1_ref[...]
    z = jnp.dot(h.astype(BF16), win_ref[...], preferred_element_type=F32)
    u = z[:, :ds]
    v = z[:, ds:ds + dc]
    g = z[:, ds + dc:]

    bu_ref[...] = jnp.dot(u.astype(BF16), bbd_ref[...], preferred_element_type=F32)

    for c in range(ns // SCAN_LANES):
        lo = c * SCAN_LANES
        re = slice(lo, lo + SCAN_LANES)
        im = slice(ns + lo, ns + lo + SCAN_LANES)
        levels = [(m_ref[2 * i, :, re], m_ref[2 * i + 1, :, re], 1 << i) for i in range(3)]
        pr = ap_ref[0, :, re]
        pi = ap_ref[1, :, re]

        def body(r, carry, re=re, im=im, levels=levels, pr=pr, pi=pi):
            cr, ci = carry
            rows = pl.ds(pl.multiple_of(r * SUBLANES, SUBLANES), SUBLANES)
            xr = bu_ref[rows, re]
            xi = bu_ref[rows, im]
            for mr, mi, shift in levels:
                sr = pltpu.roll(xr, shift, 0)
                si = pltpu.roll(xi, shift, 0)
                xr, xi = xr + mr * sr - mi * si, xi + mr * si + mi * sr
            xr, xi = xr + pr * cr - pi * ci, xi + pr * ci + pi * cr
            bu_ref[rows, re] = xr
            bu_ref[rows, im] = xi
            last = SUBLANES - 1
            return (jnp.broadcast_to(xr[last:last + 1, :], xr.shape),
                    jnp.broadcast_to(xi[last:last + 1, :], xi.shape))

        cr, ci = lax.fori_loop(0, tl // SUBLANES, body, (carry_ref[0, :, re], carry_ref[1, :, re]))
        carry_ref[0, :, re] = cr
        carry_ref[1, :, re] = ci

    y = jnp.dot(bu_ref[...].astype(BF16), cbd_ref[...], preferred_element_type=F32) + d_ref[...] * u
    yg = _gelu(y)
    gate = jnp.dot(yg.astype(BF16), glu_ref[...], preferred_element_type=F32) + glub_ref[...]
    ycat_ref[:, 0:ds] = (yg * _sigmoid(gate)).astype(BF16)

    hbuf_ref[HIST:HIST + tl, :] = v * _sigmoid(g)
    for r0 in range(0, tl, CONV_ROWS):
        acc = jnp.broadcast_to(cb_ref[...], (CONV_ROWS, dc))
        first = [HIST - (width - 1) + k for k in range(width)]
        for a in range(SUBLANES):
            part = None
            for k, f in enumerate(first):
                if f % SUBLANES == a:
                    lo = r0 + f - a
                    term = cw_ref[k:k + 1, :] * hbuf_ref[lo:lo + CONV_ROWS + SUBLANES, :]
                    part = term if part is None else part + term
            acc = acc + part[a:a + CONV_ROWS, :]
        mu = jnp.mean(acc, axis=-1, keepdims=True)
        cen = acc - mu
        var = jnp.mean(cen * cen, axis=-1, keepdims=True)
        hn = cen * lax.rsqrt(var + EPS) * lng_ref[...] + lnb_ref[...]
        ycat_ref[r0:r0 + CONV_ROWS, ds:ds + dc] = (hn * _sigmoid(hn)).astype(BF16)
    hbuf_ref[0:HIST, :] = hbuf_ref[tl:tl + HIST, :]

    o_ref[...] = x + jnp.dot(ycat_ref[...], wout_ref[...], preferred_element_type=F32)


def _s5_tables(lam_re, lam_im, log_dt, b_re, b_im, c_re, c_im, glu_w):
    ng, ns = lam_re.shape
    gh = b_re.shape[-1]
    dt = jnp.exp(log_dt)[:, None]
    mag = jnp.exp(lam_re * dt)
    ang = lam_im * dt
    a_re = mag * jnp.cos(ang)
    a_im = mag * jnp.sin(ang)
    num_re = a_re - 1.0
    num_im = a_im
    den = lam_re * lam_re + lam_im * lam_im
    f_re = (num_re * lam_re + num_im * lam_im) / den
    f_im = (num_im * lam_re - num_re * lam_im) / den
    bb_re = f_re[..., None] * b_re - f_im[..., None] * b_im
    bb_im = f_re[..., None] * b_im + f_im[..., None] * b_re
    eye = jnp.eye(ng, dtype=F32)
    to_state = lambda m: jnp.einsum('gph,gk->ghkp', m, eye).reshape(ng * gh, ng * ns)
    from_state = lambda m: jnp.einsum('ghp,gk->gpkh', m, eye).reshape(ng * ns, ng * gh)
    bbd = jnp.concatenate([to_state(bb_re), to_state(bb_im)], axis=1).astype(BF16)
    cbd = jnp.concatenate([from_state(c_re), -from_state(c_im)], axis=0).astype(BF16)
    glu_bd = jnp.einsum('ghk,gm->ghmk', glu_w, eye).reshape(ng * gh, ng * gh).astype(BF16)

    ar = a_re.reshape(1, ng * ns)
    ai = a_im.reshape(1, ng * ns)
    powers = [(ar, ai)]
    for _ in range(SUBLANES - 1):
        qr, qi = powers[-1]
        powers.append((qr * ar - qi * ai, qr * ai + qi * ar))
    row = jnp.arange(SUBLANES)[:, None]
    shift_mult = []
    for shift in (1, 2, 4):
        qr, qi = powers[shift - 1]
        shift_mult.append(jnp.where(row >= shift, qr, 0.0))
        shift_mult.append(jnp.where(row >= shift, qi, 0.0))
    shift_mult = jnp.stack(shift_mult)
    carry_mult = jnp.stack([jnp.concatenate([p[0] for p in powers], axis=0),
                            jnp.concatenate([p[1] for p in powers], axis=0)])
    return bbd, cbd, glu_bd, shift_mult, carry_mult


def _mixer(x, norm1_g, w_in, tables, ssm_d, glu_b, conv_w, conv_b, ln_g, ln_b, w_out, *, tl):
    bsz, seq, d = x.shape
    bbd, cbd, glu_bd, shift_mult, carry_mult = tables
    ds = ssm_d.shape[-1]
    dc = conv_b.shape[-1]
    ns = carry_mult.shape[-1]
    row = lambda a: a.reshape(1, -1).astype(F32)
    const = lambda a: pl.BlockSpec(a.shape, lambda b, t: (0,) * a.ndim)
    operands = [row(norm1_g), w_in.astype(BF16), bbd, cbd, shift_mult, carry_mult, row(ssm_d), glu_bd,
                row(glu_b), conv_w.astype(F32), row(conv_b), row(ln_g), row(ln_b), w_out.astype(BF16)]
    return pl.pallas_call(
        _mixer_kernel,
        out_shape=jax.ShapeDtypeStruct((bsz, seq, d), F32),
        grid=(bsz, seq // tl),
        in_specs=[pl.BlockSpec((None, tl, d), lambda b, t: (b, t, 0))] + [const(a) for a in operands],
        out_specs=pl.BlockSpec((None, tl, d), lambda b, t: (b, t, 0)),
        scratch_shapes=[pltpu.VMEM((tl, 2 * ns), F32),
                        pltpu.VMEM((2, SUBLANES, ns), F32),
                        pltpu.VMEM((tl + HIST + SUBLANES, dc), F32),
                        pltpu.VMEM((tl, ds + dc), BF16)],
        compiler_params=pltpu.CompilerParams(dimension_semantics=("arbitrary", "arbitrary"),
                                             vmem_limit_bytes=VMEM_LIMIT),
        name="mixer",
    )(x, *operands)


def _top16(vals, tie):
    width = vals.shape[1]
    slot = lax.broadcasted_iota(jnp.int32, (TOPK, width), 0)
    work = vals
    rank = jnp.full(vals.shape, NOT_SELECTED, F32)
    tops = jnp.zeros((TOPK, width), F32)
    for r in range(TOPK):
        m = jnp.max(work, axis=0, keepdims=True)
        first = jnp.min(jnp.where(work == m, tie, jnp.inf), axis=0, keepdims=True)
        hit = tie == first
        work = jnp.where(hit, -jnp.inf, work)
        rank = jnp.where(hit, float(r), rank)
        tops = jnp.where(slot == r, m, tops)
    return tops, rank


def _route_chunk(s1, s2):
    nk, width = s1.shape
    key_id = lax.broadcasted_iota(jnp.int32, (nk, width), 0).astype(F32)
    v1, r1 = _top16(s1, key_id)
    v2, r2 = _top16(s2, key_id)

    half = TOPK // 2
    pieces = [v1 + v2[j:j + 1, :] for j in range(half)] + [v1[0:1, :] + v2[half:TOPK, :]]
    cand = jnp.concatenate(pieces, axis=0)
    row = lax.broadcasted_iota(jnp.int32, cand.shape, 0)
    flat = jnp.where(row < half * TOPK, (row % TOPK) * TOPK + row // TOPK, row - half * TOPK + half)
    _, rank = _top16(cand, flat.astype(F32))
    sel = jnp.where(rank < float(TOPK), 1.0, 0.0)
    cnt = sel[0:TOPK, :]
    for j in range(1, half):
        cnt = cnt + sel[j * TOPK:(j + 1) * TOPK, :]
    tail = jnp.sum(sel[half * TOPK:, :], axis=0, keepdims=True)
    slot = lax.broadcasted_iota(jnp.int32, (TOPK, width), 0)
    cnt = jnp.where(slot == 0, cnt + tail, cnt)

    e1 = jnp.exp(v1 - v1[0:1, :])
    e2 = jnp.exp(v2 - v2[0:1, :])
    inner = jnp.zeros((TOPK, width), F32)
    for j in range(TOPK):
        inner = inner + jnp.where(cnt > float(j), e2[j:j + 1, :], 0.0)
    inv_z = 1.0 / jnp.sum(e1 * inner, axis=0, keepdims=True)

    cnt_key = jnp.zeros((nk, width), F32)
    for i in range(TOPK):
        cnt_key = cnt_key + jnp.where(r1 == float(i), cnt[i:i + 1, :], 0.0)
    f1 = jnp.exp(s1 - v1[0:1, :]) * inv_z
    f2 = jnp.exp(s2 - v2[0:1, :])
    return r2, f2, cnt_key, f1


def _oddeven_merge(lo, hi, r):
    step = r * 2
    if step < hi - lo:
        yield from _oddeven_merge(lo, hi, step)
        yield from _oddeven_merge(lo + r, hi, step)
        yield from [(i, i + r) for i in range(lo + r, hi - r, step)]
    else:
        yield (lo, lo + r)


def _oddeven_sort(lo, hi):
    if hi - lo >= 1:
        mid = lo + (hi - lo) // 2
        yield from _oddeven_sort(lo, mid)
        yield from _oddeven_sort(mid + 1, hi)
        yield from _oddeven_merge(lo, hi, 1)


SORT16 = tuple(_oddeven_sort(0, TOPK - 1))
BITONIC16 = tuple((i, i + d) for d in (8, 4, 2, 1) for i in range(TOPK) if (i & d) == 0)


def _exchange(a, pairs):
    for i, j in pairs:
        a[i], a[j] = jnp.maximum(a[i], a[j]), jnp.minimum(a[i], a[j])


def _over_sublanes(x, op):
    for shift in (4, 2, 1):
        x = op(x, pltpu.roll(x, shift, 0))
    return x


def _sorted_top16(regs):
    a = list(regs)
    _exchange(a, SORT16)
    for shift in (4, 2, 1):
        b = [pltpu.roll(x, shift, 0) for x in a]
        a = [jnp.maximum(a[v], b[TOPK - 1 - v]) for v in range(TOPK)]
        _exchange(a, BITONIC16)
    return a


def _search16(s, v):
    c1 = s < v[7]
    c2 = s < jnp.where(c1, v[11], v[3])
    c3 = s < jnp.where(c1, jnp.where(c2, v[13], v[9]), jnp.where(c2, v[5], v[1]))
    c4 = s < jnp.where(c1, jnp.where(c2, jnp.where(c3, v[14], v[12]), jnp.where(c3, v[10], v[8])),
                       jnp.where(c2, jnp.where(c3, v[6], v[4]), jnp.where(c3, v[2], v[0])))
    return s < v[15], c1, c2, c3, c4


def _mux16(c, vals):
    _, c1, c2, c3, c4 = c
    lvl = [jnp.where(c4, vals[2 * k + 1], vals[2 * k]) for k in range(8)]
    lvl = [jnp.where(c3, lvl[2 * k + 1], lvl[2 * k]) for k in range(4)]
    lvl = [jnp.where(c2, lvl[2 * k + 1], lvl[2 * k]) for k in range(2)]
    return jnp.where(c1, lvl[1], lvl[0])


def _route_chunk_fast(s1, s2):
    nk, width = s1.shape
    assert nk == TOPK * SUBLANES
    regs = lambda a: [a[r:r + SUBLANES, :] for r in range(0, nk, SUBLANES)]
    s1r, s2r = regs(s1), regs(s2)
    v1 = _sorted_top16(s1r)
    v2 = _sorted_top16(s2r)
    sub = lax.broadcasted_iota(jnp.int32, (SUBLANES, width), 0)
    one = lambda m: jnp.where(m, 1.0, 0.0)

    def column(vs):
        out = vs[0]
        for k in range(1, SUBLANES):
            out = jnp.where(sub == k, vs[k], out)
        return out

    v2lo, v2hi, v1hi = column(v2[:SUBLANES]), column(v2[SUBLANES:]), column(v1[SUBLANES:])
    cands = [v1[0] + v2lo, v1[0] + v2hi] + [v1[i] + v2lo for i in range(1, SUBLANES)] + [v1hi + v2[0]]
    work = list(cands)
    treemax = lambda ws: _over_sublanes(functools.reduce(jnp.maximum, ws), jnp.maximum)
    for _ in range(TOPK - 1):
        m = treemax(work)
        work = [jnp.where(w == m, -jnp.inf, w) for w in work]
    tau = treemax(work)
    keep = [c >= tau for c in cands]
    kf = [one(k) for k in keep]
    cnt = [_over_sublanes(kf[0] + kf[1], jnp.add)]
    cnt += [_over_sublanes(kf[i + 1], jnp.add) for i in range(1, SUBLANES)]
    cnt += [jnp.broadcast_to(kf[-1][k:k + 1, :], kf[-1].shape) for k in range(SUBLANES)]
    top = v1[0] + v2[0]
    z = _over_sublanes(functools.reduce(jnp.add, [jnp.where(k, jnp.exp(c - top), 0.0)
                                                  for k, c in zip(keep, cands)]), jnp.add)
    inv_z = 1.0 / z

    bad = one(functools.reduce(jnp.add, cnt) != float(TOPK))
    for v in (v1, v2):
        for r in range(TOPK - 1):
            bad = jnp.maximum(bad, one(v[r] == v[r + 1]))

    r2, f2, cnt_key, f1 = [], [], [], []
    above1 = jnp.zeros((SUBLANES, width), F32)
    above2 = jnp.zeros((SUBLANES, width), F32)
    for s, t in zip(s1r, s2r):
        c = _search16(s, v1)
        cnt_key.append(jnp.where(c[0], 0.0, _mux16(c, cnt)))
        f1.append(jnp.exp(s - v1[0]) * inv_z)
        above1 = above1 + jnp.where(c[0], 0.0, 1.0)
        c = _search16(t, v2)
        rank = (jnp.where(c[1], 8.0, 0.0) + jnp.where(c[2], 4.0, 0.0)) + (jnp.where(c[3], 2.0, 0.0)
                                                                          + jnp.where(c[4], 1.0, 0.0))
        r2.append(jnp.where(c[0], NOT_SELECTED, rank))
        f2.append(jnp.exp(t - v2[0]))
        above2 = above2 + jnp.where(c[0], 0.0, 1.0)
    for above in (above1, above2):
        bad = jnp.maximum(bad, one(_over_sublanes(above, jnp.add) != float(TOPK)))
    cat = lambda rs: jnp.concatenate(rs, axis=0)
    return cat(r2), cat(f2), cat(cnt_key), cat(f1), bad


def _route_kernel(x2_ref, g2_ref, wqt_ref, k1_ref, k2_ref,
                  xnt_ref, r2_ref, f2_ref, cnt_ref, f1_ref, qt_ref):
    head = pl.program_id(1)
    nk = k1_ref.shape[0]
    half = k1_ref.shape[1]
    tb = x2_ref.shape[0]

    @pl.when(head == 0)
    def _():
        x = x2_ref[...]
        xn = x * _rms_scale(x) * g2_ref[...]
        xnt = xn.T.astype(BF16)
        xnt_ref[...] = xnt
        qt_ref[...] = jnp.dot(wqt_ref[...], xnt, preferred_element_type=F32).astype(BF16)

    base = pl.multiple_of(head * (2 * half), 2 * half)
    s1 = jnp.dot(k1_ref[...], qt_ref[pl.ds(base, half), :], preferred_element_type=F32)
    s2 = jnp.dot(k2_ref[...], qt_ref[pl.ds(base + half, half), :], preferred_element_type=F32)
    def store(cols, r2, f2, cnt_key, f1):
        r2_ref[:, cols] = r2.astype(BF16)
        f2_ref[:, cols] = f2.astype(BF16)
        cnt_ref[:, cols] = cnt_key
        f1_ref[:, cols] = f1

    chunks = [slice(c * LANES, (c + 1) * LANES) for c in range(tb // LANES)]
    bad = jnp.zeros((SUBLANES, LANES), F32)
    for cols in chunks:
        *tables, tie = _route_chunk_fast(s1[:, cols], s2[:, cols])
        store(cols, *tables)
        bad = jnp.maximum(bad, tie)

    @pl.when(jnp.max(bad) > 0.0)
    def _():
        for cols in chunks:
            store(cols, *_route_chunk(s1[:, cols], s2[:, cols]))
    del nk


def _route(x2, norm2_g, wq_t, k1, k2, *, tb, heads):
    n, d = x2.shape
    nk, half = k1.shape
    tok = lambda dt: jax.ShapeDtypeStruct((heads, nk, n), dt)
    per_head = pl.BlockSpec((None, nk, tb), lambda i, h: (h, 0, i))
    const = lambda a: pl.BlockSpec(a.shape, lambda i, h: (0,) * a.ndim)
    g2 = norm2_g.reshape(1, d).astype(F32)
    return pl.pallas_call(
        _route_kernel,
        out_shape=(jax.ShapeDtypeStruct((d, n), BF16), tok(BF16), tok(BF16), tok(F32), tok(F32)),
        grid=(n // tb, heads),
        in_specs=[pl.BlockSpec((tb, d), lambda i, h: (i, 0)), const(g2), const(wq_t), const(k1), const(k2)],
        out_specs=(pl.BlockSpec((d, tb), lambda i, h: (0, i)), per_head, per_head, per_head, per_head),
        scratch_shapes=[pltpu.VMEM((wq_t.shape[0], tb), BF16)],
        compiler_params=pltpu.CompilerParams(dimension_semantics=("arbitrary", "arbitrary"),
                                             vmem_limit_bytes=VMEM_LIMIT),
        name="route",
    )(x2, g2, wq_t, k1, k2)


def _peer_kernel(xnt_ref, u_ref, vt_ref, r2_ref, f2_ref, cnt_ref, f1_ref, x2_ref, gf_ref, o_ref,
                 acc_ref, at_ref, wt_ref):
    j = pl.program_id(1)
    heads, nk, tm = r2_ref.shape
    eb = u_ref.shape[0]
    blocks = eb // nk

    @pl.when(j == 0)
    def _():
        acc_ref[...] = jnp.zeros(acc_ref.shape, F32)

    at_ref[...] = jnp.dot(u_ref[...], xnt_ref[...], preferred_element_type=F32)
    for b in range(blocks):
        e1 = j * blocks + b
        rows = slice(b * nk, (b + 1) * nk)
        act = _gelu(at_ref[rows, :]).astype(BF16)
        tile_rows = 2 * SUBLANES
        tiled = (nk // tile_rows, tile_rows, tm)
        row = lambda ref, h: jnp.broadcast_to(ref[h, pl.ds(e1, 1), :], (tile_rows, tm)).astype(BF16)[None]
        gate = jnp.zeros(tiled, BF16)
        for h in range(heads):
            kept = jnp.where(r2_ref[h].reshape(tiled) < row(cnt_ref, h), f2_ref[h].reshape(tiled),
                             jnp.zeros((), BF16))
            gate = gate + kept * row(f1_ref, h)
        wt_ref[rows, :] = act * gate.reshape(nk, tm)
    acc_ref[...] += jnp.dot(vt_ref[...], wt_ref[...], preferred_element_type=F32)

    @pl.when(j == pl.num_programs(1) - 1)
    def _():
        y = x2_ref[...] + acc_ref[...].T
        o_ref[...] = y * _rms_scale(y) * gf_ref[...]


def _peer(xnt, u_bf, vt_bf, r2, f2, cnt, f1, x2, final_g, *, tm, eb):
    d, n = xnt.shape
    ne = u_bf.shape[0]
    heads, nk, _ = r2.shape
    gf = final_g.reshape(1, d).astype(F32)
    tables = pl.BlockSpec((heads, nk, tm), lambda i, j: (0, 0, i))
    return pl.pallas_call(
        _peer_kernel,
        out_shape=jax.ShapeDtypeStruct((n, d), F32),
        grid=(n // tm, ne // eb),
        in_specs=[pl.BlockSpec((d, tm), lambda i, j: (0, i)),
                  pl.BlockSpec((eb, d), lambda i, j: (j, 0)),
                  pl.BlockSpec((d, eb), lambda i, j: (0, j)),
                  tables, tables, tables, tables,
                  pl.BlockSpec((tm, d), lambda i, j: (i, 0)),
                  pl.BlockSpec((1, d), lambda i, j: (0, 0))],
        out_specs=pl.BlockSpec((tm, d), lambda i, j: (i, 0)),
        scratch_shapes=[pltpu.VMEM((d, tm), F32), pltpu.VMEM((eb, tm), F32), pltpu.VMEM((eb, tm), BF16)],
        compiler_params=pltpu.CompilerParams(dimension_semantics=("arbitrary", "arbitrary"),
                                             vmem_limit_bytes=VMEM_LIMIT),
        name="peer",
    )(xnt, u_bf, vt_bf, r2, f2, cnt, f1, x2, gf)


def _layer(x, norm1_g, w_in, lam_re, lam_im, log_dt, b_re, b_im, c_re, c_im, ssm_d, glu_w, glu_b,
           conv_w, conv_b, ln_g, ln_b, w_out, norm2_g, wq, k1, k2, u_tab, v_tab, out_g,
           *, tl, tb, tm, eb):
    bsz, seq, d = x.shape
    heads = wq.shape[1] // (2 * k1.shape[1])
    tables = _s5_tables(lam_re, lam_im, log_dt, b_re, b_im, c_re, c_im, glu_w)
    x2 = _mixer(x, norm1_g, w_in, tables, ssm_d, glu_b, conv_w, conv_b, ln_g, ln_b, w_out, tl=tl)
    x2 = x2.reshape(bsz * seq, d)
    xnt, r2, f2, cnt, f1 = _route(x2, norm2_g, wq.T.astype(BF16), k1.astype(BF16), k2.astype(BF16),
                                  tb=tb, heads=heads)
    out = _peer(xnt, u_tab.astype(BF16), v_tab.T.astype(BF16), r2, f2, cnt, f1, x2, out_g, tm=tm, eb=eb)
    return out.reshape(bsz, seq, d)


def kernel(x, norm1_g, w_in, lam_re, lam_im, log_dt, ssm_b_re, ssm_b_im, ssm_c_re, ssm_c_im, ssm_d, glu_w,
           glu_b, conv_w, conv_b, conv_ln_g, conv_ln_b, w_out, norm2_g, peer_wq, peer_k1, peer_k2, peer_u,
           peer_v, final_g):
    depth = w_in.shape[0]
    d = x.shape[-1]
    for l in range(depth):
        last = l == depth - 1
        assert last, "only the single-layer configuration is implemented"
        x = _layer(x, norm1_g[l], w_in[l], lam_re[l], lam_im[l], log_dt[l], ssm_b_re[l], ssm_b_im[l],
                   ssm_c_re[l], ssm_c_im[l], ssm_d[l], glu_w[l], glu_b[l], conv_w[l], conv_b[l],
                   conv_ln_g[l], conv_ln_b[l], w_out[l], norm2_g[l], peer_wq[l], peer_k1[l], peer_k2[l],
                   peer_u[l], peer_v[l], final_g, tl=256, tb=256, tm=512, eb=1024)
    del d
    return x
```

```python
import functools
import math

import jax
import jax.numpy as jnp
from jax import lax
from jax.experimental import pallas as pl
from jax.experimental.pallas import tpu as pltpu

F32 = jnp.float32
BF16 = jnp.bfloat16

EPS = 1e-6
TOPK = 16
SUBLANES = 8
LANES = 128
VMEM_LIMIT = 56 * 1024 * 1024
NOT_SELECTED = 127.0


def _gelu(x):
    return 0.5 * x * (1.0 + lax.erf(x * (1.0 / math.sqrt(2.0))))


def _sigmoid(x):
    return 1.0 / (1.0 + jnp.exp(-x))


def _rms_scale(x):
    return lax.rsqrt(jnp.mean(x * x, axis=-1, keepdims=True) + EPS)


SCAN_LANES = 256
SCAN_UNROLL = 4
CONV_ROWS = 32
HIST = 32


def _mixer_kernel(x_ref, g1_ref, win_ref, bbd_ref, cbd_ref, m_ref, ap_ref, d_ref, glu_ref, glub_ref,
                  cw_ref, cb_ref, lng_ref, lnb_ref, wout_ref, o_ref,
                  bu_ref, st_ref, carry_ref, hbuf_ref, ycat_ref):
    tl = x_ref.shape[0]
    ns = carry_ref.shape[-1]
    ds = d_ref.shape[-1]
    dc = cb_ref.shape[-1]
    width = cw_ref.shape[0]

    @pl.when(pl.program_id(1) == 0)
    def _():
        carry_ref[...] = jnp.zeros(carry_ref.shape, F32)
        hbuf_ref[0:HIST, :] = jnp.zeros((HIST, dc), F32)
        hbuf_ref[HIST + tl:HIST + tl + SUBLANES, :] = jnp.zeros((SUBLANES, dc), F32)

    x = x_ref[...]
    h = x * _rms_scale(x) * g1_ref[...]
    z = jnp.dot(h.astype(BF16), win_ref[...], preferred_element_type=F32)
    u = z[:, :ds]
    v = z[:, ds:ds + dc]
    g = z[:, ds + dc:]

    bu_ref[...] = jnp.dot(u.astype(BF16), bbd_ref[...], preferred_element_type=F32)

    for c in range(ns // SCAN_LANES):
        lo = c * SCAN_LANES
        re = slice(lo, lo + SCAN_LANES)
        im = slice(ns + lo, ns + lo + SCAN_LANES)
        levels = [(m_ref[2 * i, :, re], m_ref[2 * i + 1, :, re], 1 << i) for i in range(3)]
        pr = ap_ref[0, :, re]
        pi = ap_ref[1, :, re]

        def body(r, carry, re=re, im=im, levels=levels, pr=pr, pi=pi):
            cr, ci = carry
            rows = pl.ds(pl.multiple_of(r * SUBLANES, SUBLANES), SUBLANES)
            xr = bu_ref[rows, re]
            xi = bu_ref[rows, im]
            for mr, mi, shift in levels:
                sr = pltpu.roll(xr, shift, 0)
                si = pltpu.roll(xi, shift, 0)
                xr, xi = xr + mr * sr - mi * si, xi + mr * si + mi * sr
            xr, xi = xr + pr * cr - pi * ci, xi + pr * ci + pi * cr
            st_ref[rows, re] = xr
            st_ref[rows, im] = xi
            last = SUBLANES - 1
            return (jnp.broadcast_to(xr[last:last + 1, :], xr.shape),
                    jnp.broadcast_to(xi[last:last + 1, :], xi.shape))

        cr, ci = lax.fori_loop(0, tl // SUBLANES, body, (carry_ref[0, :, re], carry_ref[1, :, re]),
                               unroll=SCAN_UNROLL)
        carry_ref[0, :, re] = cr
        carry_ref[1, :, re] = ci

    y = jnp.dot(st_ref[...].astype(BF16), cbd_ref[...], preferred_element_type=F32) + d_ref[...] * u
    yg = _gelu(y)
    gate = jnp.dot(yg.astype(BF16), glu_ref[...], preferred_element_type=F32) + glub_ref[...]
    ycat_ref[:, 0:ds] = (yg * _sigmoid(gate)).astype(BF16)

    hbuf_ref[HIST:HIST + tl, :] = v * _sigmoid(g)
    for r0 in range(0, tl, CONV_ROWS):
        acc = jnp.broadcast_to(cb_ref[...], (CONV_ROWS, dc))
        first = [HIST - (width - 1) + k for k in range(width)]
        for a in range(SUBLANES):
            part = None
            for k, f in enumerate(first):
                if f % SUBLANES == a:
                    lo = r0 + f - a
                    term = cw_ref[k:k + 1, :] * hbuf_ref[lo:lo + CONV_ROWS + SUBLANES, :]
                    part = term if part is None else part + term
            acc = acc + part[a:a + CONV_ROWS, :]
        mu = jnp.mean(acc, axis=-1, keepdims=True)
        cen = acc - mu
        var = jnp.mean(cen * cen, axis=-1, keepdims=True)
        hn = cen * lax.rsqrt(var + EPS) * lng_ref[...] + lnb_ref[...]
        ycat_ref[r0:r0 + CONV_ROWS, ds:ds + dc] = (hn * _sigmoid(hn)).astype(BF16)
    hbuf_ref[0:HIST, :] = hbuf_ref[tl:tl + HIST, :]

    o_ref[...] = x + jnp.dot(ycat_ref[...], wout_ref[...], preferred_element_type=F32)


def _s5_tables(lam_re, lam_im, log_dt, b_re, b_im, c_re, c_im, glu_w):
    ng, ns = lam_re.shape
    gh = b_re.shape[-1]
    dt = jnp.exp(log_dt)[:, None]
    mag = jnp.exp(lam_re * dt)
    ang = lam_im * dt
    a_re = mag * jnp.cos(ang)
    a_im = mag * jnp.sin(ang)
    num_re = a_re - 1.0
    num_im = a_im
    den = lam_re * lam_re + lam_im * lam_im
    f_re = (num_re * lam_re + num_im * lam_im) / den
    f_im = (num_im * lam_re - num_re * lam_im) / den
    bb_re = f_re[..., None] * b_re - f_im[..., None] * b_im
    bb_im = f_re[..., None] * b_im + f_im[..., None] * b_re
    eye = jnp.eye(ng, dtype=F32)
    to_state = lambda m: jnp.einsum('gph,gk->ghkp', m, eye).reshape(ng * gh, ng * ns)
    from_state = lambda m: jnp.einsum('ghp,gk->gpkh', m, eye).reshape(ng * ns, ng * gh)
    bbd = jnp.concatenate([to_state(bb_re), to_state(bb_im)], axis=1).astype(BF16)
    cbd = jnp.concatenate([from_state(c_re), -from_state(c_im)], axis=0).astype(BF16)
    glu_bd = jnp.einsum('ghk,gm->ghmk', glu_w, eye).reshape(ng * gh, ng * gh).astype(BF16)

    ar = a_re.reshape(1, ng * ns)
    ai = a_im.reshape(1, ng * ns)
    powers = [(ar, ai)]
    for _ in range(SUBLANES - 1):
        qr, qi = powers[-1]
        powers.append((qr * ar - qi * ai, qr * ai + qi * ar))
    row = jnp.arange(SUBLANES)[:, None]
    shift_mult = []
    for shift in (1, 2, 4):
        qr, qi = powers[shift - 1]
        shift_mult.append(jnp.where(row >= shift, qr, 0.0))
        shift_mult.append(jnp.where(row >= shift, qi, 0.0))
    shift_mult = jnp.stack(shift_mult)
    carry_mult = jnp.stack([jnp.concatenate([p[0] for p in powers], axis=0),
                            jnp.concatenate([p[1] for p in powers], axis=0)])
    return bbd, cbd, glu_bd, shift_mult, carry_mult


def _mixer(x, norm1_g, w_in, tables, ssm_d, glu_b, conv_w, conv_b, ln_g, ln_b, w_out, *, tl):
    bsz, seq, d = x.shape
    bbd, cbd, glu_bd, shift_mult, carry_mult = tables
    ds = ssm_d.shape[-1]
    dc = conv_b.shape[-1]
    ns = carry_mult.shape[-1]
    row = lambda a: a.reshape(1, -1).astype(F32)
    const = lambda a: pl.BlockSpec(a.shape, lambda b, t: (0,) * a.ndim)
    operands = [row(norm1_g), w_in.astype(BF16), bbd, cbd, shift_mult, carry_mult, row(ssm_d), glu_bd,
                row(glu_b), conv_w.astype(F32), row(conv_b), row(ln_g), row(ln_b), w_out.astype(BF16)]
    return pl.pallas_call(
        _mixer_kernel,
        out_shape=jax.ShapeDtypeStruct((bsz, seq, d), F32),
        grid=(bsz, seq // tl),
        in_specs=[pl.BlockSpec((None, tl, d), lambda b, t: (b, t, 0))] + [const(a) for a in operands],
        out_specs=pl.BlockSpec((None, tl, d), lambda b, t: (b, t, 0)),
        scratch_shapes=[pltpu.VMEM((tl, 2 * ns), F32),
                        pltpu.VMEM((tl, 2 * ns), F32),
                        pltpu.VMEM((2, SUBLANES, ns), F32),
                        pltpu.VMEM((tl + HIST + SUBLANES, dc), F32),
                        pltpu.VMEM((tl, ds + dc), BF16)],
        compiler_params=pltpu.CompilerParams(dimension_semantics=("arbitrary", "arbitrary"),
                                             vmem_limit_bytes=VMEM_LIMIT),
        name="mixer",
    )(x, *operands)


def _top16(vals, tie):
    width = vals.shape[1]
    slot = lax.broadcasted_iota(jnp.int32, (TOPK, width), 0)
    work = vals
    rank = jnp.full(vals.shape, NOT_SELECTED, F32)
    tops = jnp.zeros((TOPK, width), F32)
    for r in range(TOPK):
        m = jnp.max(work, axis=0, keepdims=True)
        first = jnp.min(jnp.where(work == m, tie, jnp.inf), axis=0, keepdims=True)
        hit = tie == first
        work = jnp.where(hit, -jnp.inf, work)
        rank = jnp.where(hit, float(r), rank)
        tops = jnp.where(slot == r, m, tops)
    return tops, rank


def _route_chunk(s1, s2):
    nk, width = s1.shape
    key_id = lax.broadcasted_iota(jnp.int32, (nk, width), 0).astype(F32)
    v1, r1 = _top16(s1, key_id)
    v2, r2 = _top16(s2, key_id)

    half = TOPK // 2
    pieces = [v1 + v2[j:j + 1, :] for j in range(half)] + [v1[0:1, :] + v2[half:TOPK, :]]
    cand = jnp.concatenate(pieces, axis=0)
    row = lax.broadcasted_iota(jnp.int32, cand.shape, 0)
    flat = jnp.where(row < half * TOPK, (row % TOPK) * TOPK + row // TOPK, row - half * TOPK + half)
    _, rank = _top16(cand, flat.astype(F32))
    sel = jnp.where(rank < float(TOPK), 1.0, 0.0)
    cnt = sel[0:TOPK, :]
    for j in range(1, half):
        cnt = cnt + sel[j * TOPK:(j + 1) * TOPK, :]
    tail = jnp.sum(sel[half * TOPK:, :], axis=0, keepdims=True)
    slot = lax.broadcasted_iota(jnp.int32, (TOPK, width), 0)
    cnt = jnp.where(slot == 0, cnt + tail, cnt)

    e1 = jnp.exp(v1 - v1[0:1, :])
    e2 = jnp.exp(v2 - v2[0:1, :])
    inner = jnp.zeros((TOPK, width), F32)
    for j in range(TOPK):
        inner = inner + jnp.where(cnt > float(j), e2[j:j + 1, :], 0.0)
    inv_z = 1.0 / jnp.sum(e1 * inner, axis=0, keepdims=True)

    cnt_key = jnp.zeros((nk, width), F32)
    for i in range(TOPK):
        cnt_key = cnt_key + jnp.where(r1 == float(i), cnt[i:i + 1, :], 0.0)
    f1 = jnp.exp(s1 - v1[0:1, :]) * inv_z
    f2 = jnp.exp(s2 - v2[0:1, :])
    return r2, f2, cnt_key, f1


def _oddeven_merge(lo, hi, r):
    step = r * 2
    if step < hi - lo:
        yield from _oddeven_merge(lo, hi, step)
        yield from _oddeven_merge(lo + r, hi, step)
        yield from [(i, i + r) for i in range(lo + r, hi - r, step)]
    else:
        yield (lo, lo + r)


def _oddeven_sort(lo, hi):
    if hi - lo >= 1:
        mid = lo + (hi - lo) // 2
        yield from _oddeven_sort(lo, mid)
        yield from _oddeven_sort(mid + 1, hi)
        yield from _oddeven_merge(lo, hi, 1)


SORT16 = tuple(_oddeven_sort(0, TOPK - 1))
BITONIC16 = tuple((i, i + d) for d in (8, 4, 2, 1) for i in range(TOPK) if (i & d) == 0)


def _exchange(a, pairs):
    for i, j in pairs:
        a[i], a[j] = jnp.maximum(a[i], a[j]), jnp.minimum(a[i], a[j])


def _over_sublanes(x, op):
    for shift in (4, 2, 1):
        x = op(x, pltpu.roll(x, shift, 0))
    return x


def _sorted_top16(regs):
    a = list(regs)
    _exchange(a, SORT16)
    for shift in (4, 2, 1):
        b = [pltpu.roll(x, shift, 0) for x in a]
        a = [jnp.maximum(a[v], b[TOPK - 1 - v]) for v in range(TOPK)]
        _exchange(a, BITONIC16)
    return a


def _search16(s, v):
    c1 = s < v[7]
    c2 = s < jnp.where(c1, v[11], v[3])
    c3 = s < jnp.where(c1, jnp.where(c2, v[13], v[9]), jnp.where(c2, v[5], v[1]))
    c4 = s < jnp.where(c1, jnp.where(c2, jnp.where(c3, v[14], v[12]), jnp.where(c3, v[10], v[8])),
                       jnp.where(c2, jnp.where(c3, v[6], v[4]), jnp.where(c3, v[2], v[0])))
    return s < v[15], c1, c2, c3, c4


def _mux16(c, vals):
    _, c1, c2, c3, c4 = c
    lvl = [jnp.where(c4, vals[2 * k + 1], vals[2 * k]) for k in range(8)]
    lvl = [jnp.where(c3, lvl[2 * k + 1], lvl[2 * k]) for k in range(4)]
    lvl = [jnp.where(c2, lvl[2 * k + 1], lvl[2 * k]) for k in range(2)]
    return jnp.where(c1, lvl[1], lvl[0])


def _route_chunk_fast(s1, s2):
    nk, width = s1.shape
    assert nk == TOPK * SUBLANES
    regs = lambda a: [a[r:r + SUBLANES, :] for r in range(0, nk, SUBLANES)]
    s1r, s2r = regs(s1), regs(s2)
    v1 = _sorted_top16(s1r)
    v2 = _sorted_top16(s2r)
    sub = lax.broadcasted_iota(jnp.int32, (SUBLANES, width), 0)
    one = lambda m: jnp.where(m, 1.0, 0.0)

    def column(vs):
        out = vs[0]
        for k in range(1, SUBLANES):
            out = jnp.where(sub == k, vs[k], out)
        return out

    v2lo, v2hi, v1hi = column(v2[:SUBLANES]), column(v2[SUBLANES:]), column(v1[SUBLANES:])
    cands = [v1[0] + v2lo, v1[0] + v2hi] + [v1[i] + v2lo for i in range(1, SUBLANES)] + [v1hi + v2[0]]
    work = list(cands)
    treemax = lambda ws: _over_sublanes(functools.reduce(jnp.maximum, ws), jnp.maximum)
    for _ in range(TOPK - 1):
        m = treemax(work)
        work = [jnp.where(w == m, -jnp.inf, w) for w in work]
    tau = treemax(work)
    keep = [c >= tau for c in cands]
    kf = [one(k) for k in keep]
    cnt = [_over_sublanes(kf[0] + kf[1], jnp.add)]
    cnt += [_over_sublanes(kf[i + 1], jnp.add) for i in range(1, SUBLANES)]
    cnt += [jnp.broadcast_to(kf[-1][k:k + 1, :], kf[-1].shape) for k in range(SUBLANES)]
    top = v1[0] + v2[0]
    z = _over_sublanes(functools.reduce(jnp.add, [jnp.where(k, jnp.exp(c - top), 0.0)
                                                  for k, c in zip(keep, cands)]), jnp.add)
    inv_z = 1.0 / z

    bad = one(functools.reduce(jnp.add, cnt) != float(TOPK))
    for v in (v1, v2):
        for r in range(TOPK - 1):
            bad = jnp.maximum(bad, one(v[r] == v[r + 1]))

    r2, f2, cnt_key, f1 = [], [], [], []
    above1 = jnp.zeros((SUBLANES, width), F32)
    above2 = jnp.zeros((SUBLANES, width), F32)
    for s, t in zip(s1r, s2r):
        c = _search16(s, v1)
        cnt_key.append(jnp.where(c[0], 0.0, _mux16(c, cnt)))
        f1.append(jnp.exp(s - v1[0]) * inv_z)
        above1 = above1 + jnp.where(c[0], 0.0, 1.0)
        c = _search16(t, v2)
        rank = (jnp.where(c[1], 8.0, 0.0) + jnp.where(c[2], 4.0, 0.0)) + (jnp.where(c[3], 2.0, 0.0)
                                                                          + jnp.where(c[4], 1.0, 0.0))
        r2.append(jnp.where(c[0], NOT_SELECTED, rank))
        f2.append(jnp.exp(t - v2[0]))
        above2 = above2 + jnp.where(c[0], 0.0, 1.0)
    for above in (above1, above2):
        bad = jnp.maximum(bad, one(_over_sublanes(above, jnp.add) != float(TOPK)))
    cat = lambda rs: jnp.concatenate(rs, axis=0)
    return cat(r2), cat(f2), cat(cnt_key), cat(f1), bad


def _route_kernel(x2_ref, g2_ref, wqt_ref, k1_ref, k2_ref,
                  xnt_ref, r2_ref, f2_ref, cnt_ref, f1_ref, qt_ref):
    head = pl.program_id(1)
    nk = k1_ref.shape[0]
    half = k1_ref.shape[1]
    tb = x2_ref.shape[0]

    @pl.when(head == 0)
    def _():
        x = x2_ref[...]
        xn = x * _rms_scale(x) * g2_ref[...]
        xnt = xn.T.astype(BF16)
        xnt_ref[...] = xnt
        qt_ref[...] = jnp.dot(wqt_ref[...], xnt, preferred_element_type=F32).astype(BF16)

    base = pl.multiple_of(head * (2 * half), 2 * half)
    s1 = jnp.dot(k1_ref[...], qt_ref[pl.ds(base, half), :], preferred_element_type=F32)
    s2 = jnp.dot(k2_ref[...], qt_ref[pl.ds(base + half, half), :], preferred_element_type=F32)
    def store(cols, r2, f2, cnt_key, f1):
        r2_ref[:, cols] = r2.astype(BF16)
        f2_ref[:, cols] = f2.astype(BF16)
        cnt_ref[:, cols] = cnt_key
        f1_ref[:, cols] = f1

    chunks = [slice(c * LANES, (c + 1) * LANES) for c in range(tb // LANES)]
    bad = jnp.zeros((SUBLANES, LANES), F32)
    for cols in chunks:
        *tables, tie = _route_chunk_fast(s1[:, cols], s2[:, cols])
        store(cols, *tables)
        bad = jnp.maximum(bad, tie)

    @pl.when(jnp.max(bad) > 0.0)
    def _():
        for cols in chunks:
            store(cols, *_route_chunk(s1[:, cols], s2[:, cols]))
    del nk


def _route(x2, norm2_g, wq_t, k1, k2, *, tb, heads):
    n, d = x2.shape
    nk, half = k1.shape
    tok = lambda dt: jax.ShapeDtypeStruct((heads, nk, n), dt)
    per_head = pl.BlockSpec((None, nk, tb), lambda i, h: (h, 0, i))
    const = lambda a: pl.BlockSpec(a.shape, lambda i, h: (0,) * a.ndim)
    g2 = norm2_g.reshape(1, d).astype(F32)
    return pl.pallas_call(
        _route_kernel,
        out_shape=(jax.ShapeDtypeStruct((d, n), BF16), tok(BF16), tok(BF16), tok(F32), tok(F32)),
        grid=(n // tb, heads),
        in_specs=[pl.BlockSpec((tb, d), lambda i, h: (i, 0)), const(g2), const(wq_t), const(k1), const(k2)],
        out_specs=(pl.BlockSpec((d, tb), lambda i, h: (0, i)), per_head, per_head, per_head, per_head),
        scratch_shapes=[pltpu.VMEM((wq_t.shape[0], tb), BF16)],
        compiler_params=pltpu.CompilerParams(dimension_semantics=("arbitrary", "arbitrary"),
                                             vmem_limit_bytes=VMEM_LIMIT),
        name="route",
    )(x2, g2, wq_t, k1, k2)


def _peer_kernel(xnt_ref, u_ref, vt_ref, r2_ref, f2_ref, cnt_ref, f1_ref, x2_ref, gf_ref, o_ref,
                 acc_ref, at_ref, wt_ref):
    j = pl.program_id(1)
    heads, nk, tm = r2_ref.shape
    eb = u_ref.shape[0]
    blocks = eb // nk

    @pl.when(j == 0)
    def _():
        acc_ref[...] = jnp.zeros(acc_ref.shape, F32)

    at_ref[...] = jnp.dot(u_ref[...], xnt_ref[...], preferred_element_type=F32)
    for b in range(blocks):
        e1 = j * blocks + b
        rows = slice(b * nk, (b + 1) * nk)
        act = _gelu(at_ref[rows, :]).astype(BF16)
        tile_rows = 2 * SUBLANES
        tiled = (nk // tile_rows, tile_rows, tm)
        row = lambda ref, h: jnp.broadcast_to(ref[h, pl.ds(e1, 1), :], (tile_rows, tm)).astype(BF16)[None]
        gate = jnp.zeros(tiled, BF16)
        for h in range(heads):
            kept = jnp.where(r2_ref[h].reshape(tiled) < row(cnt_ref, h), f2_ref[h].reshape(tiled),
                             jnp.zeros((), BF16))
            gate = gate + kept * row(f1_ref, h)
        wt_ref[rows, :] = act * gate.reshape(nk, tm)
    acc_ref[...] += jnp.dot(vt_ref[...], wt_ref[...], preferred_element_type=F32)

    @pl.when(j == pl.num_programs(1) - 1)
    def _():
        y = x2_ref[...] + acc_ref[...].T
        o_ref[...] = y * _rms_scale(y) * gf_ref[...]


def _peer(xnt, u_bf, vt_bf, r2, f2, cnt, f1, x2, final_g, *, tm, eb):
    d, n = xnt.shape
    ne = u_bf.shape[0]
    heads, nk, _ = r2.shape
    gf = final_g.reshape(1, d).astype(F32)
    tables = pl.BlockSpec((heads, nk, tm), lambda i, j: (0, 0, i))
    return pl.pallas_call(
        _peer_kernel,
        out_shape=jax.ShapeDtypeStruct((n, d), F32),
        grid=(n // tm, ne // eb),
        in_specs=[pl.BlockSpec((d, tm), lambda i, j: (0, i)),
                  pl.BlockSpec((eb, d), lambda i, j: (j, 0)),
                  pl.BlockSpec((d, eb), lambda i, j: (0, j)),
                  tables, tables, tables, tables,
                  pl.BlockSpec((tm, d), lambda i, j: (i, 0)),
                  pl.BlockSpec((1, d), lambda i, j: (0, 0))],
        out_specs=pl.BlockSpec((tm, d), lambda i, j: (i, 0)),
        scratch_shapes=[pltpu.VMEM((d, tm), F32), pltpu.VMEM((eb, tm), F32), pltpu.VMEM((eb, tm), BF16)],
        compiler_params=pltpu.CompilerParams(dimension_semantics=("arbitrary", "arbitrary"),
                                             vmem_limit_bytes=VMEM_LIMIT),
        name="peer",
    )(xnt, u_bf, vt_bf, r2, f2, cnt, f1, x2, gf)


def _layer(x, norm1_g, w_in, lam_re, lam_im, log_dt, b_re, b_im, c_re, c_im, ssm_d, glu_w, glu_b,
           conv_w, conv_b, ln_g, ln_b, w_out, norm2_g, wq, k1, k2, u_tab, v_tab, out_g,
           *, tl, tb, tm, eb):
    bsz, seq, d = x.shape
    heads = wq.shape[1] // (2 * k1.shape[1])
    tables = _s5_tables(lam_re, lam_im, log_dt, b_re, b_im, c_re, c_im, glu_w)
    x2 = _mixer(x, norm1_g, w_in, tables, ssm_d, glu_b, conv_w, conv_b, ln_g, ln_b, w_out, tl=tl)
    x2 = x2.reshape(bsz * seq, d)
    xnt, r2, f2, cnt, f1 = _route(x2, norm2_g, wq.T.astype(BF16), k1.astype(BF16), k2.astype(BF16),
                                  tb=tb, heads=heads)
    out = _peer(xnt, u_tab.astype(BF16), v_tab.T.astype(BF16), r2, f2, cnt, f1, x2, out_g, tm=tm, eb=eb)
    return out.reshape(bsz, seq, d)


def kernel(x, norm1_g, w_in, lam_re, lam_im, log_dt, ssm_b_re, ssm_b_im, ssm_c_re, ssm_c_im, ssm_d, glu_w,
           glu_b, conv_w, conv_b, conv_ln_g, conv_ln_b, w_out, norm2_g, peer_wq, peer_k1, peer_k2, peer_u,
           peer_v, final_g):
    depth = w_in.shape[0]
    d = x.shape[-1]
    for l in range(depth):
        last = l == depth - 1
        assert last, "only the single-layer configuration is implemented"
        x = _layer(x, norm1_g[l], w_in[l], lam_re[l], lam_im[l], log_dt[l], ssm_b_re[l], ssm_b_im[l],
                   ssm_c_re[l], ssm_c_im[l], ssm_d[l], glu_w[l], glu_b[l], conv_w[l], conv_b[l],
                   conv_ln_g[l], conv_ln_b[l], w_out[l], norm2_g[l], peer_wq[l], peer_k1[l], peer_k2[l],
                   peer_u[l], peer_v[l], final_g, tl=256, tb=256, tm=512, eb=1024)
    del d
    return x
```

```python
import functools
import math

import jax
import jax.numpy as jnp
from jax import lax
from jax.experimental import pallas as pl
from jax.experimental.pallas import tpu as pltpu

F32 = jnp.float32
BF16 = jnp.bfloat16

EPS = 1e-6
TOPK = 16
SUBLANES = 8
LANES = 128
VMEM_LIMIT = 56 * 1024 * 1024
NOT_SELECTED = 127.0


def _gelu(x):
    return 0.5 * x * (1.0 + lax.erf(x * (1.0 / math.sqrt(2.0))))


def _sigmoid(x):
    return 1.0 / (1.0 + jnp.exp(-x))


def _rms_scale(x):
    return lax.rsqrt(jnp.mean(x * x, axis=-1, keepdims=True) + EPS)


SCAN_LANES = 256
SCAN_UNROLL = 4
CONV_ROWS = 32
HIST = 32


def _mixer_kernel(x_ref, g1_ref, win_ref, bbd_ref, cbd_ref, m_ref, ap_ref, d_ref, glu_ref, glub_ref,
                  cw_ref, cb_ref, lng_ref, lnb_ref, wout_ref, o_ref,
                  bu_ref, st_ref, carry_ref, hbuf_ref, ycat_ref):
    tl = x_ref.shape[0]
    ns = carry_ref.shape[-1]
    ds = d_ref.shape[-1]
    dc = cb_ref.shape[-1]
    width = cw_ref.shape[0]

    @pl.when(pl.program_id(1) == 0)
    def _():
        carry_ref[...] = jnp.zeros(carry_ref.shape, F32)
        hbuf_ref[0:HIST, :] = jnp.zeros((HIST, dc), F32)
        hbuf_ref[HIST + tl:HIST + tl + SUBLANES, :] = jnp.zeros((SUBLANES, dc), F32)

    x = x_ref[...]
    h = x * _rms_scale(x) * g1_ref[...]
    z = jnp.dot(h.astype(BF16), win_ref[...], preferred_element_type=F32)
    u = z[:, :ds]
    v = z[:, ds:ds + dc]
    g = z[:, ds + dc:]

    bu_ref[...] = jnp.dot(u.astype(BF16), bbd_ref[...], preferred_element_type=F32)

    for c in range(ns // SCAN_LANES):
        lo = c * SCAN_LANES
        re = slice(lo, lo + SCAN_LANES)
        im = slice(ns + lo, ns + lo + SCAN_LANES)
        levels = [(m_ref[2 * i, :, re], m_ref[2 * i + 1, :, re], 1 << i) for i in range(3)]
        pr = ap_ref[0, :, re]
        pi = ap_ref[1, :, re]

        def body(r, carry, re=re, im=im, levels=levels, pr=pr, pi=pi):
            cr, ci = carry
            rows = pl.ds(pl.multiple_of(r * SUBLANES, SUBLANES), SUBLANES)
            xr = bu_ref[rows, re]
            xi = bu_ref[rows, im]
            for mr, mi, shift in levels:
                sr = pltpu.roll(xr, shift, 0)
                si = pltpu.roll(xi, shift, 0)
                xr, xi = xr + mr * sr - mi * si, xi + mr * si + mi * sr
            xr, xi = xr + pr * cr - pi * ci, xi + pr * ci + pi * cr
            st_ref[rows, re] = xr
            st_ref[rows, im] = xi
            last = SUBLANES - 1
            return (jnp.broadcast_to(xr[last:last + 1, :], xr.shape),
                    jnp.broadcast_to(xi[last:last + 1, :], xi.shape))

        cr, ci = lax.fori_loop(0, tl // SUBLANES, body, (carry_ref[0, :, re], carry_ref[1, :, re]),
                               unroll=SCAN_UNROLL)
        carry_ref[0, :, re] = cr
        carry_ref[1, :, re] = ci

    y = jnp.dot(st_ref[...].astype(BF16), cbd_ref[...], preferred_element_type=F32) + d_ref[...] * u
    yg = _gelu(y)
    gate = jnp.dot(yg.astype(BF16), glu_ref[...], preferred_element_type=F32) + glub_ref[...]
    ycat_ref[:, 0:ds] = (yg * _sigmoid(gate)).astype(BF16)

    hbuf_ref[HIST:HIST + tl, :] = v * _sigmoid(g)
    for r0 in range(0, tl, CONV_ROWS):
        acc = jnp.broadcast_to(cb_ref[...], (CONV_ROWS, dc))
        first = [HIST - (width - 1) + k for k in range(width)]
        for a in range(SUBLANES):
            part = None
            for k, f in enumerate(first):
                if f % SUBLANES == a:
                    lo = r0 + f - a
                    term = cw_ref[k:k + 1, :] * hbuf_ref[lo:lo + CONV_ROWS + SUBLANES, :]
                    part = term if part is None else part + term
            acc = acc + part[a:a + CONV_ROWS, :]
        mu = jnp.mean(acc, axis=-1, keepdims=True)
        cen = acc - mu
        var = jnp.mean(cen * cen, axis=-1, keepdims=True)
        hn = cen * lax.rsqrt(var + EPS) * lng_ref[...] + lnb_ref[...]
        ycat_ref[r0:r0 + CONV_ROWS, ds:ds + dc] = (hn * _sigmoid(hn)).astype(BF16)
    hbuf_ref[0:HIST, :] = hbuf_ref[tl:tl + HIST, :]

    o_ref[...] = x + jnp.dot(ycat_ref[...], wout_ref[...], preferred_element_type=F32)


def _s5_tables(lam_re, lam_im, log_dt, b_re, b_im, c_re, c_im, glu_w):
    ng, ns = lam_re.shape
    gh = b_re.shape[-1]
    dt = jnp.exp(log_dt)[:, None]
    mag = jnp.exp(lam_re * dt)
    ang = lam_im * dt
    a_re = mag * jnp.cos(ang)
    a_im = mag * jnp.sin(ang)
    num_re = a_re - 1.0
    num_im = a_im
    den = lam_re * lam_re + lam_im * lam_im
    f_re = (num_re * lam_re + num_im * lam_im) / den
    f_im = (num_im * lam_re - num_re * lam_im) / den
    bb_re = f_re[..., None] * b_re - f_im[..., None] * b_im
    bb_im = f_re[..., None] * b_im + f_im[..., None] * b_re
    eye = jnp.eye(ng, dtype=F32)
    to_state = lambda m: jnp.einsum('gph,gk->ghkp', m, eye).reshape(ng * gh, ng * ns)
    from_state = lambda m: jnp.einsum('ghp,gk->gpkh', m, eye).reshape(ng * ns, ng * gh)
    bbd = jnp.concatenate([to_state(bb_re), to_state(bb_im)], axis=1).astype(BF16)
    cbd = jnp.concatenate([from_state(c_re), -from_state(c_im)], axis=0).astype(BF16)
    glu_bd = jnp.einsum('ghk,gm->ghmk', glu_w, eye).reshape(ng * gh, ng * gh).astype(BF16)

    ar = a_re.reshape(1, ng * ns)
    ai = a_im.reshape(1, ng * ns)
    powers = [(ar, ai)]
    for _ in range(SUBLANES - 1):
        qr, qi = powers[-1]
        powers.append((qr * ar - qi * ai, qr * ai + qi * ar))
    row = jnp.arange(SUBLANES)[:, None]
    shift_mult = []
    for shift in (1, 2, 4):
        qr, qi = powers[shift - 1]
        shift_mult.append(jnp.where(row >= shift, qr, 0.0))
        shift_mult.append(jnp.where(row >= shift, qi, 0.0))
    shift_mult = jnp.stack(shift_mult)
    carry_mult = jnp.stack([jnp.concatenate([p[0] for p in powers], axis=0),
                            jnp.concatenate([p[1] for p in powers], axis=0)])
    return bbd, cbd, glu_bd, shift_mult, carry_mult


def _mixer(x, norm1_g, w_in, tables, ssm_d, glu_b, conv_w, conv_b, ln_g, ln_b, w_out, *, tl):
    bsz, seq, d = x.shape
    bbd, cbd, glu_bd, shift_mult, carry_mult = tables
    ds = ssm_d.shape[-1]
    dc = conv_b.shape[-1]
    ns = carry_mult.shape[-1]
    row = lambda a: a.reshape(1, -1).astype(F32)
    const = lambda a: pl.BlockSpec(a.shape, lambda b, t: (0,) * a.ndim)
    operands = [row(norm1_g), w_in.astype(BF16), bbd, cbd, shift_mult, carry_mult, row(ssm_d), glu_bd,
                row(glu_b), conv_w.astype(F32), row(conv_b), row(ln_g), row(ln_b), w_out.astype(BF16)]
    return pl.pallas_call(
        _mixer_kernel,
        out_shape=jax.ShapeDtypeStruct((bsz, seq, d), F32),
        grid=(bsz, seq // tl),
        in_specs=[pl.BlockSpec((None, tl, d), lambda b, t: (b, t, 0))] + [const(a) for a in operands],
        out_specs=pl.BlockSpec((None, tl, d), lambda b, t: (b, t, 0)),
        scratch_shapes=[pltpu.VMEM((tl, 2 * ns), F32),
                        pltpu.VMEM((tl, 2 * ns), F32),
                        pltpu.VMEM((2, SUBLANES, ns), F32),
                        pltpu.VMEM((tl + HIST + SUBLANES, dc), F32),
                        pltpu.VMEM((tl, ds + dc), BF16)],
        compiler_params=pltpu.CompilerParams(dimension_semantics=("arbitrary", "arbitrary"),
                                             vmem_limit_bytes=VMEM_LIMIT),
        name="mixer",
    )(x, *operands)


def _top16(vals, tie):
    width = vals.shape[1]
    slot = lax.broadcasted_iota(jnp.int32, (TOPK, width), 0)
    work = vals
    rank = jnp.full(vals.shape, NOT_SELECTED, F32)
    tops = jnp.zeros((TOPK, width), F32)
    for r in range(TOPK):
        m = jnp.max(work, axis=0, keepdims=True)
        first = jnp.min(jnp.where(work == m, tie, jnp.inf), axis=0, keepdims=True)
        hit = tie == first
        work = jnp.where(hit, -jnp.inf, work)
        rank = jnp.where(hit, float(r), rank)
        tops = jnp.where(slot == r, m, tops)
    return tops, rank


def _route_chunk(s1, s2):
    nk, width = s1.shape
    key_id = lax.broadcasted_iota(jnp.int32, (nk, width), 0).astype(F32)
    v1, r1 = _top16(s1, key_id)
    v2, r2 = _top16(s2, key_id)

    half = TOPK // 2
    pieces = [v1 + v2[j:j + 1, :] for j in range(half)] + [v1[0:1, :] + v2[half:TOPK, :]]
    cand = jnp.concatenate(pieces, axis=0)
    row = lax.broadcasted_iota(jnp.int32, cand.shape, 0)
    flat = jnp.where(row < half * TOPK, (row % TOPK) * TOPK + row // TOPK, row - half * TOPK + half)
    _, rank = _top16(cand, flat.astype(F32))
    sel = jnp.where(rank < float(TOPK), 1.0, 0.0)
    cnt = sel[0:TOPK, :]
    for j in range(1, half):
        cnt = cnt + sel[j * TOPK:(j + 1) * TOPK, :]
    tail = jnp.sum(sel[half * TOPK:, :], axis=0, keepdims=True)
    slot = lax.broadcasted_iota(jnp.int32, (TOPK, width), 0)
    cnt = jnp.where(slot == 0, cnt + tail, cnt)

    e1 = jnp.exp(v1 - v1[0:1, :])
    e2 = jnp.exp(v2 - v2[0:1, :])
    inner = jnp.zeros((TOPK, width), F32)
    for j in range(TOPK):
        inner = inner + jnp.where(cnt > float(j), e2[j:j + 1, :], 0.0)
    inv_z = 1.0 / jnp.sum(e1 * inner, axis=0, keepdims=True)

    cnt_key = jnp.zeros((nk, width), F32)
    for i in range(TOPK):
        cnt_key = cnt_key + jnp.where(r1 == float(i), cnt[i:i + 1, :], 0.0)
    f1 = jnp.exp(s1 - v1[0:1, :]) * inv_z
    f2 = jnp.exp(s2 - v2[0:1, :])
    return r2, f2, cnt_key, f1


def _oddeven_merge(lo, hi, r):
    step = r * 2
    if step < hi - lo:
        yield from _oddeven_merge(lo, hi, step)
        yield from _oddeven_merge(lo + r, hi, step)
        yield from [(i, i + r) for i in range(lo + r, hi - r, step)]
    else:
        yield (lo, lo + r)


def _oddeven_sort(lo, hi):
    if hi - lo >= 1:
        mid = lo + (hi - lo) // 2
        yield from _oddeven_sort(lo, mid)
        yield from _oddeven_sort(mid + 1, hi)
        yield from _oddeven_merge(lo, hi, 1)


SORT16 = tuple(_oddeven_sort(0, TOPK - 1))
BITONIC16 = tuple((i, i + d) for d in (8, 4, 2, 1) for i in range(TOPK) if (i & d) == 0)


def _exchange(a, pairs):
    for i, j in pairs:
        a[i], a[j] = jnp.maximum(a[i], a[j]), jnp.minimum(a[i], a[j])


def _over_sublanes(x, op):
    for shift in (4, 2, 1):
        x = op(x, pltpu.roll(x, shift, 0))
    return x


def _sorted_top16(regs):
    a = list(regs)
    _exchange(a, SORT16)
    for shift in (4, 2, 1):
        b = [pltpu.roll(x, shift, 0) for x in a]
        a = [jnp.maximum(a[v], b[TOPK - 1 - v]) for v in range(TOPK)]
        _exchange(a, BITONIC16)
    return a


def _search16(s, v):
    c1 = s < v[7]
    c2 = s < jnp.where(c1, v[11], v[3])
    c3 = s < jnp.where(c1, jnp.where(c2, v[13], v[9]), jnp.where(c2, v[5], v[1]))
    c4 = s < jnp.where(c1, jnp.where(c2, jnp.where(c3, v[14], v[12]), jnp.where(c3, v[10], v[8])),
                       jnp.where(c2, jnp.where(c3, v[6], v[4]), jnp.where(c3, v[2], v[0])))
    return s < v[15], c1, c2, c3, c4


def _mux16(c, vals):
    _, c1, c2, c3, c4 = c
    lvl = [jnp.where(c4, vals[2 * k + 1], vals[2 * k]) for k in range(8)]
    lvl = [jnp.where(c3, lvl[2 * k + 1], lvl[2 * k]) for k in range(4)]
    lvl = [jnp.where(c2, lvl[2 * k + 1], lvl[2 * k]) for k in range(2)]
    return jnp.where(c1, lvl[1], lvl[0])


def _route_chunk_fast(s1, s2):
    nk, width = s1.shape
    assert nk == TOPK * SUBLANES
    regs = lambda a: [a[r:r + SUBLANES, :] for r in range(0, nk, SUBLANES)]
    s1r, s2r = regs(s1), regs(s2)
    v1 = _sorted_top16(s1r)
    v2 = _sorted_top16(s2r)
    sub = lax.broadcasted_iota(jnp.int32, (SUBLANES, width), 0)
    one = lambda m: jnp.where(m, 1.0, 0.0)

    def column(vs):
        out = vs[0]
        for k in range(1, SUBLANES):
            out = jnp.where(sub == k, vs[k], out)
        return out

    v2lo, v2hi, v1hi = column(v2[:SUBLANES]), column(v2[SUBLANES:]), column(v1[SUBLANES:])
    cands = [v1[0] + v2lo, v1[0] + v2hi] + [v1[i] + v2lo for i in range(1, SUBLANES)] + [v1hi + v2[0]]
    work = list(cands)
    treemax = lambda ws: _over_sublanes(functools.reduce(jnp.maximum, ws), jnp.maximum)
    for _ in range(TOPK - 1):
        m = treemax(work)
        work = [jnp.where(w == m, -jnp.inf, w) for w in work]
    tau = treemax(work)
    keep = [c >= tau for c in cands]
    kf = [one(k) for k in keep]
    cnt = [_over_sublanes(kf[0] + kf[1], jnp.add)]
    cnt += [_over_sublanes(kf[i + 1], jnp.add) for i in range(1, SUBLANES)]
    cnt += [jnp.broadcast_to(kf[-1][k:k + 1, :], kf[-1].shape) for k in range(SUBLANES)]
    top = v1[0] + v2[0]
    z = _over_sublanes(functools.reduce(jnp.add, [jnp.where(k, jnp.exp(c - top), 0.0)
                                                  for k, c in zip(keep, cands)]), jnp.add)
    inv_z = 1.0 / z

    bad = one(functools.reduce(jnp.add, cnt) != float(TOPK))
    for v in (v1, v2):
        for r in range(TOPK - 1):
            bad = jnp.maximum(bad, one(v[r] == v[r + 1]))

    r2, f2, cnt_key, f1 = [], [], [], []
    above1 = jnp.zeros((SUBLANES, width), F32)
    above2 = jnp.zeros((SUBLANES, width), F32)
    for s, t in zip(s1r, s2r):
        c = _search16(s, v1)
        cnt_key.append(jnp.where(c[0], 0.0, _mux16(c, cnt)))
        f1.append(jnp.exp(s - v1[0]) * inv_z)
        above1 = above1 + jnp.where(c[0], 0.0, 1.0)
        c = _search16(t, v2)
        rank = (jnp.where(c[1], 8.0, 0.0) + jnp.where(c[2], 4.0, 0.0)) + (jnp.where(c[3], 2.0, 0.0)
                                                                          + jnp.where(c[4], 1.0, 0.0))
        r2.append(jnp.where(c[0], NOT_SELECTED, rank))
        f2.append(jnp.exp(t - v2[0]))
        above2 = above2 + jnp.where(c[0], 0.0, 1.0)
    for above in (above1, above2):
        bad = jnp.maximum(bad, one(_over_sublanes(above, jnp.add) != float(TOPK)))
    cat = lambda rs: jnp.concatenate(rs, axis=0)
    return cat(r2), cat(f2), cat(cnt_key), cat(f1), bad


def _route_kernel(x2_ref, g2_ref, wqt_ref, k1_ref, k2_ref,
                  xnt_ref, r2_ref, f2_ref, cnt_ref, f1_ref, qt_ref):
    head = pl.program_id(1)
    nk = k1_ref.shape[0]
    half = k1_ref.shape[1]
    tb = x2_ref.shape[0]

    @pl.when(head == 0)
    def _():
        x = x2_ref[...]
        xn = x * _rms_scale(x) * g2_ref[...]
        xnt = xn.T.astype(BF16)
        xnt_ref[...] = xnt
        qt_ref[...] = jnp.dot(wqt_ref[...], xnt, preferred_element_type=F32).astype(BF16)

    base = pl.multiple_of(head * (2 * half), 2 * half)
    s1 = jnp.dot(k1_ref[...], qt_ref[pl.ds(base, half), :], preferred_element_type=F32)
    s2 = jnp.dot(k2_ref[...], qt_ref[pl.ds(base + half, half), :], preferred_element_type=F32)
    def store(cols, r2, f2, cnt_key, f1):
        r2_ref[:, cols] = r2.astype(BF16)
        f2_ref[:, cols] = f2.astype(BF16)
        cnt_ref[:, cols] = cnt_key
        f1_ref[:, cols] = f1

    chunks = [slice(c * LANES, (c + 1) * LANES) for c in range(tb // LANES)]
    bad = jnp.zeros((SUBLANES, LANES), F32)
    for cols in chunks:
        *tables, tie = _route_chunk_fast(s1[:, cols], s2[:, cols])
        store(cols, *tables)
        bad = jnp.maximum(bad, tie)

    @pl.when(jnp.max(bad) > 0.0)
    def _():
        for cols in chunks:
            store(cols, *_route_chunk(s1[:, cols], s2[:, cols]))
    del nk


def _route(x2, norm2_g, wq_t, k1, k2, *, tb, heads):
    n, d = x2.shape
    nk, half = k1.shape
    tok = lambda dt: jax.ShapeDtypeStruct((heads, nk, n), dt)
    per_head = pl.BlockSpec((None, nk, tb), lambda i, h: (h, 0, i))
    const = lambda a: pl.BlockSpec(a.shape, lambda i, h: (0,) * a.ndim)
    g2 = norm2_g.reshape(1, d).astype(F32)
    return pl.pallas_call(
        _route_kernel,
        out_shape=(jax.ShapeDtypeStruct((d, n), BF16), tok(BF16), tok(BF16), tok(F32), tok(F32)),
        grid=(n // tb, heads),
        in_specs=[pl.BlockSpec((tb, d), lambda i, h: (i, 0)), const(g2), const(wq_t), const(k1), const(k2)],
        out_specs=(pl.BlockSpec((d, tb), lambda i, h: (0, i)), per_head, per_head, per_head, per_head),
        scratch_shapes=[pltpu.VMEM((wq_t.shape[0], tb), BF16)],
        compiler_params=pltpu.CompilerParams(dimension_semantics=("arbitrary", "arbitrary"),
                                             vmem_limit_bytes=VMEM_LIMIT),
        name="route",
    )(x2, g2, wq_t, k1, k2)


def _peer_kernel(xnt_ref, u_ref, vt_ref, r2_ref, f2_ref, cnt_ref, f1_ref, x2_ref, gf_ref, o_ref,
                 acc_ref, at_ref, wt_ref):
    j = pl.program_id(1)
    heads, nk, tm = r2_ref.shape
    eb = u_ref.shape[0]
    blocks = eb // nk

    @pl.when(j == 0)
    def _():
        acc_ref[...] = jnp.zeros(acc_ref.shape, F32)

    at_ref[...] = jnp.dot(u_ref[...], xnt_ref[...], preferred_element_type=F32)
    for b in range(blocks):
        e1 = j * blocks + b
        rows = slice(b * nk, (b + 1) * nk)
        act = _gelu(at_ref[rows, :]).astype(BF16)
        tile_rows = 2 * SUBLANES
        tiled = (nk // tile_rows, tile_rows, tm)
        row = lambda ref, h: jnp.broadcast_to(ref[h, pl.ds(e1, 1), :], (tile_rows, tm)).astype(BF16)[None]
        gate = jnp.zeros(tiled, BF16)
        for h in range(heads):
            kept = jnp.where(r2_ref[h].reshape(tiled) < row(cnt_ref, h), f2_ref[h].reshape(tiled),
                             jnp.zeros((), BF16))
            gate = gate + kept * row(f1_ref, h)
        wt_ref[rows, :] = act * gate.reshape(nk, tm)
    acc_ref[...] += jnp.dot(vt_ref[...], wt_ref[...], preferred_element_type=F32)

    @pl.when(j == pl.num_programs(1) - 1)
    def _():
        y = x2_ref[...] + acc_ref[...].T
        o_ref[...] = y * _rms_scale(y) * gf_ref[...]


def _peer(xnt, u_bf, vt_bf, r2, f2, cnt, f1, x2, final_g, *, tm, eb):
    d, n = xnt.shape
    ne = u_bf.shape[0]
    heads, nk, _ = r2.shape
    gf = final_g.reshape(1, d).astype(F32)
    tables = pl.BlockSpec((heads, nk, tm), lambda i, j: (0, 0, i))
    return pl.pallas_call(
        _peer_kernel,
        out_shape=jax.ShapeDtypeStruct((n, d), F32),
        grid=(n // tm, ne // eb),
        in_specs=[pl.BlockSpec((d, tm), lambda i, j: (0, i)),
                  pl.BlockSpec((eb, d), lambda i, j: (j, 0)),
                  pl.BlockSpec((d, eb), lambda i, j: (0, j)),
                  tables, tables, tables, tables,
                  pl.BlockSpec((tm, d), lambda i, j: (i, 0)),
                  pl.BlockSpec((1, d), lambda i, j: (0, 0))],
        out_specs=pl.BlockSpec((tm, d), lambda i, j: (i, 0)),
        scratch_shapes=[pltpu.VMEM((d, tm), F32), pltpu.VMEM((eb, tm), F32), pltpu.VMEM((eb, tm), BF16)],
        compiler_params=pltpu.CompilerParams(dimension_semantics=("arbitrary", "arbitrary"),
                                             vmem_limit_bytes=VMEM_LIMIT),
        name="peer",
    )(xnt, u_bf, vt_bf, r2, f2, cnt, f1, x2, gf)


def _layer(x, norm1_g, w_in, lam_re, lam_im, log_dt, b_re, b_im, c_re, c_im, ssm_d, glu_w, glu_b,
           conv_w, conv_b, ln_g, ln_b, w_out, norm2_g, wq, k1, k2, u_tab, v_tab, out_g,
           *, tl, tb, tm, eb):
    bsz, seq, d = x.shape
    heads = wq.shape[1] // (2 * k1.shape[1])
    tables = _s5_tables(lam_re, lam_im, log_dt, b_re, b_im, c_re, c_im, glu_w)
    x2 = _mixer(x, norm1_g, w_in, tables, ssm_d, glu_b, conv_w, conv_b, ln_g, ln_b, w_out, tl=tl)
    x2 = x2.reshape(bsz * seq, d)
    xnt, r2, f2, cnt, f1 = _route(x2, norm2_g, wq.T.astype(BF16), k1.astype(BF16), k2.astype(BF16),
                                  tb=tb, heads=heads)
    out = _peer(xnt, u_tab.astype(BF16), v_tab.T.astype(BF16), r2, f2, cnt, f1, x2, out_g, tm=tm, eb=eb)
    return out.reshape(bsz, seq, d)


def kernel(x, norm1_g, w_in, lam_re, lam_im, log_dt, ssm_b_re, ssm_b_im, ssm_c_re, ssm_c_im, ssm_d, glu_w,
           glu_b, conv_w, conv_b, conv_ln_g, conv_ln_b, w_out, norm2_g, peer_wq, peer_k1, peer_k2, peer_u,
           peer_v, final_g):
    depth = w_in.shape[0]
    d = x.shape[-1]
    for l in range(depth):
        last = l == depth - 1
        assert last, "only the single-layer configuration is implemented"
        x = _layer(x, norm1_g[l], w_in[l], lam_re[l], lam_im[l], log_dt[l], ssm_b_re[l], ssm_b_im[l],
                   ssm_c_re[l], ssm_c_im[l], ssm_d[l], glu_w[l], glu_b[l], conv_w[l], conv_b[l],
                   conv_ln_g[l], conv_ln_b[l], w_out[l], norm2_g[l], peer_wq[l], peer_k1[l], peer_k2[l],
                   peer_u[l], peer_v[l], final_g, tl=256, tb=256, tm=512, eb=2048)
    del d
    return x
```

```python
import functools
import math

import jax
import jax.numpy as jnp
from jax import lax
from jax.experimental import pallas as pl
from jax.experimental.pallas import tpu as pltpu

F32 = jnp.float32
BF16 = jnp.bfloat16

EPS = 1e-6
TOPK = 16
SUBLANES = 8
LANES = 128
VMEM_LIMIT = 56 * 1024 * 1024
NOT_SELECTED = 127.0


def _gelu(x):
    return 0.5 * x * (1.0 + lax.erf(x * (1.0 / math.sqrt(2.0))))


def _sigmoid(x):
    return 1.0 / (1.0 + jnp.exp(-x))


def _rms_scale(x):
    return lax.rsqrt(jnp.mean(x * x, axis=-1, keepdims=True) + EPS)


SCAN_LANES = 256
SCAN_UNROLL = 4
CONV_ROWS = 32
HIST = 32


def _mixer_kernel(x_ref, g1_ref, win_ref, bbd_ref, cbd_ref, m_ref, ap_ref, d_ref, glu_ref, glub_ref,
                  cw_ref, cb_ref, lng_ref, lnb_ref, wout_ref, o_ref,
                  bu_ref, st_ref, carry_ref, hbuf_ref, ycat_ref):
    tl = x_ref.shape[0]
    ns = carry_ref.shape[-1]
    ds = d_ref.shape[-1]
    dc = cb_ref.shape[-1]
    width = cw_ref.shape[0]

    @pl.when(pl.program_id(1) == 0)
    def _():
        carry_ref[...] = jnp.zeros(carry_ref.shape, F32)
        hbuf_ref[0:HIST, :] = jnp.zeros((HIST, dc), F32)
        hbuf_ref[HIST + tl:HIST + tl + SUBLANES, :] = jnp.zeros((SUBLANES, dc), F32)

    x = x_ref[...]
    h = x * _rms_scale(x) * g1_ref[...]
    z = jnp.dot(h.astype(BF16), win_ref[...], preferred_element_type=F32)
    u = z[:, :ds]
    v = z[:, ds:ds + dc]
    g = z[:, ds + dc:]

    bu_ref[...] = jnp.dot(u.astype(BF16), bbd_ref[...], preferred_element_type=F32)

    for c in range(ns // SCAN_LANES):
        lo = c * SCAN_LANES
        re = slice(lo, lo + SCAN_LANES)
        im = slice(ns + lo, ns + lo + SCAN_LANES)
        levels = [(m_ref[2 * i, :, re], m_ref[2 * i + 1, :, re], 1 << i) for i in range(3)]
        pr = ap_ref[0, :, re]
        pi = ap_ref[1, :, re]

        def body(r, carry, re=re, im=im, levels=levels, pr=pr, pi=pi):
            cr, ci = carry
            rows = pl.ds(pl.multiple_of(r * SUBLANES, SUBLANES), SUBLANES)
            xr = bu_ref[rows, re]
            xi = bu_ref[rows, im]
            for mr, mi, shift in levels:
                sr = pltpu.roll(xr, shift, 0)
                si = pltpu.roll(xi, shift, 0)
                xr, xi = xr + mr * sr - mi * si, xi + mr * si + mi * sr
            xr, xi = xr + pr * cr - pi * ci, xi + pr * ci + pi * cr
            st_ref[rows, re] = xr
            st_ref[rows, im] = xi
            last = SUBLANES - 1
            return (jnp.broadcast_to(xr[last:last + 1, :], xr.shape),
                    jnp.broadcast_to(xi[last:last + 1, :], xi.shape))

        cr, ci = lax.fori_loop(0, tl // SUBLANES, body, (carry_ref[0, :, re], carry_ref[1, :, re]),
                               unroll=SCAN_UNROLL)
        carry_ref[0, :, re] = cr
        carry_ref[1, :, re] = ci

    y = jnp.dot(st_ref[...].astype(BF16), cbd_ref[...], preferred_element_type=F32) + d_ref[...] * u
    yg = _gelu(y)
    gate = jnp.dot(yg.astype(BF16), glu_ref[...], preferred_element_type=F32) + glub_ref[...]
    ycat_ref[:, 0:ds] = (yg * _sigmoid(gate)).astype(BF16)

    hbuf_ref[HIST:HIST + tl, :] = v * _sigmoid(g)
    for r0 in range(0, tl, CONV_ROWS):
        acc = jnp.broadcast_to(cb_ref[...], (CONV_ROWS, dc))
        first = [HIST - (width - 1) + k for k in range(width)]
        for a in range(SUBLANES):
            part = None
            for k, f in enumerate(first):
                if f % SUBLANES == a:
                    lo = r0 + f - a
                    term = cw_ref[k:k + 1, :] * hbuf_ref[lo:lo + CONV_ROWS + SUBLANES, :]
                    part = term if part is None else part + term
            acc = acc + part[a:a + CONV_ROWS, :]
        mu = jnp.mean(acc, axis=-1, keepdims=True)
        cen = acc - mu
        var = jnp.mean(cen * cen, axis=-1, keepdims=True)
        hn = cen * lax.rsqrt(var + EPS) * lng_ref[...] + lnb_ref[...]
        ycat_ref[r0:r0 + CONV_ROWS, ds:ds + dc] = (hn * _sigmoid(hn)).astype(BF16)
    hbuf_ref[0:HIST, :] = hbuf_ref[tl:tl + HIST, :]

    o_ref[...] = x + jnp.dot(ycat_ref[...], wout_ref[...], preferred_element_type=F32)


def _s5_tables(lam_re, lam_im, log_dt, b_re, b_im, c_re, c_im, glu_w):
    ng, ns = lam_re.shape
    gh = b_re.shape[-1]
    dt = jnp.exp(log_dt)[:, None]
    mag = jnp.exp(lam_re * dt)
    ang = lam_im * dt
    a_re = mag * jnp.cos(ang)
    a_im = mag * jnp.sin(ang)
    num_re = a_re - 1.0
    num_im = a_im
    den = lam_re * lam_re + lam_im * lam_im
    f_re = (num_re * lam_re + num_im * lam_im) / den
    f_im = (num_im * lam_re - num_re * lam_im) / den
    bb_re = f_re[..., None] * b_re - f_im[..., None] * b_im
    bb_im = f_re[..., None] * b_im + f_im[..., None] * b_re
    eye = jnp.eye(ng, dtype=F32)
    to_state = lambda m: jnp.einsum('gph,gk->ghkp', m, eye).reshape(ng * gh, ng * ns)
    from_state = lambda m: jnp.einsum('ghp,gk->gpkh', m, eye).reshape(ng * ns, ng * gh)
    bbd = jnp.concatenate([to_state(bb_re), to_state(bb_im)], axis=1).astype(BF16)
    cbd = jnp.concatenate([from_state(c_re), -from_state(c_im)], axis=0).astype(BF16)
    glu_bd = jnp.einsum('ghk,gm->ghmk', glu_w, eye).reshape(ng * gh, ng * gh).astype(BF16)

    ar = a_re.reshape(1, ng * ns)
    ai = a_im.reshape(1, ng * ns)
    powers = [(ar, ai)]
    for _ in range(SUBLANES - 1):
        qr, qi = powers[-1]
        powers.append((qr * ar - qi * ai, qr * ai + qi * ar))
    row = jnp.arange(SUBLANES)[:, None]
    shift_mult = []
    for shift in (1, 2, 4):
        qr, qi = powers[shift - 1]
        shift_mult.append(jnp.where(row >= shift, qr, 0.0))
        shift_mult.append(jnp.where(row >= shift, qi, 0.0))
    shift_mult = jnp.stack(shift_mult)
    carry_mult = jnp.stack([jnp.concatenate([p[0] for p in powers], axis=0),
                            jnp.concatenate([p[1] for p in powers], axis=0)])
    return bbd, cbd, glu_bd, shift_mult, carry_mult


def _mixer(x, norm1_g, w_in, tables, ssm_d, glu_b, conv_w, conv_b, ln_g, ln_b, w_out, *, tl):
    bsz, seq, d = x.shape
    bbd, cbd, glu_bd, shift_mult, carry_mult = tables
    ds = ssm_d.shape[-1]
    dc = conv_b.shape[-1]
    ns = carry_mult.shape[-1]
    row = lambda a: a.reshape(1, -1).astype(F32)
    const = lambda a: pl.BlockSpec(a.shape, lambda b, t: (0,) * a.ndim)
    operands = [row(norm1_g), w_in.astype(BF16), bbd, cbd, shift_mult, carry_mult, row(ssm_d), glu_bd,
                row(glu_b), conv_w.astype(F32), row(conv_b), row(ln_g), row(ln_b), w_out.astype(BF16)]
    return pl.pallas_call(
        _mixer_kernel,
        out_shape=jax.ShapeDtypeStruct((bsz, seq, d), F32),
        grid=(bsz, seq // tl),
        in_specs=[pl.BlockSpec((None, tl, d), lambda b, t: (b, t, 0))] + [const(a) for a in operands],
        out_specs=pl.BlockSpec((None, tl, d), lambda b, t: (b, t, 0)),
        scratch_shapes=[pltpu.VMEM((tl, 2 * ns), F32),
                        pltpu.VMEM((tl, 2 * ns), F32),
                        pltpu.VMEM((2, SUBLANES, ns), F32),
                        pltpu.VMEM((tl + HIST + SUBLANES, dc), F32),
                        pltpu.VMEM((tl, ds + dc), BF16)],
        compiler_params=pltpu.CompilerParams(dimension_semantics=("arbitrary", "arbitrary"),
                                             vmem_limit_bytes=VMEM_LIMIT),
        name="mixer",
    )(x, *operands)


def _top16(vals, tie):
    width = vals.shape[1]
    slot = lax.broadcasted_iota(jnp.int32, (TOPK, width), 0)
    work = vals
    rank = jnp.full(vals.shape, NOT_SELECTED, F32)
    tops = jnp.zeros((TOPK, width), F32)
    for r in range(TOPK):
        m = jnp.max(work, axis=0, keepdims=True)
        first = jnp.min(jnp.where(work == m, tie, jnp.inf), axis=0, keepdims=True)
        hit = tie == first
        work = jnp.where(hit, -jnp.inf, work)
        rank = jnp.where(hit, float(r), rank)
        tops = jnp.where(slot == r, m, tops)
    return tops, rank


def _route_chunk(s1, s2):
    nk, width = s1.shape
    key_id = lax.broadcasted_iota(jnp.int32, (nk, width), 0).astype(F32)
    v1, r1 = _top16(s1, key_id)
    v2, r2 = _top16(s2, key_id)

    half = TOPK // 2
    pieces = [v1 + v2[j:j + 1, :] for j in range(half)] + [v1[0:1, :] + v2[half:TOPK, :]]
    cand = jnp.concatenate(pieces, axis=0)
    row = lax.broadcasted_iota(jnp.int32, cand.shape, 0)
    flat = jnp.where(row < half * TOPK, (row % TOPK) * TOPK + row // TOPK, row - half * TOPK + half)
    _, rank = _top16(cand, flat.astype(F32))
    sel = jnp.where(rank < float(TOPK), 1.0, 0.0)
    cnt = sel[0:TOPK, :]
    for j in range(1, half):
        cnt = cnt + sel[j * TOPK:(j + 1) * TOPK, :]
    tail = jnp.sum(sel[half * TOPK:, :], axis=0, keepdims=True)
    slot = lax.broadcasted_iota(jnp.int32, (TOPK, width), 0)
    cnt = jnp.where(slot == 0, cnt + tail, cnt)

    e1 = jnp.exp(v1 - v1[0:1, :])
    e2 = jnp.exp(v2 - v2[0:1, :])
    inner = jnp.zeros((TOPK, width), F32)
    for j in range(TOPK):
        inner = inner + jnp.where(cnt > float(j), e2[j:j + 1, :], 0.0)
    inv_z = 1.0 / jnp.sum(e1 * inner, axis=0, keepdims=True)

    cnt_key = jnp.zeros((nk, width), F32)
    for i in range(TOPK):
        cnt_key = cnt_key + jnp.where(r1 == float(i), cnt[i:i + 1, :], 0.0)
    f1 = jnp.exp(s1 - v1[0:1, :]) * inv_z
    f2 = jnp.exp(s2 - v2[0:1, :])
    return r2, f2, cnt_key, f1


def _oddeven_merge(lo, hi, r):
    step = r * 2
    if step < hi - lo:
        yield from _oddeven_merge(lo, hi, step)
        yield from _oddeven_merge(lo + r, hi, step)
        yield from [(i, i + r) for i in range(lo + r, hi - r, step)]
    else:
        yield (lo, lo + r)


def _oddeven_sort(lo, hi):
    if hi - lo >= 1:
        mid = lo + (hi - lo) // 2
        yield from _oddeven_sort(lo, mid)
        yield from _oddeven_sort(mid + 1, hi)
        yield from _oddeven_merge(lo, hi, 1)


SORT16 = tuple(_oddeven_sort(0, TOPK - 1))
BITONIC16 = tuple((i, i + d) for d in (8, 4, 2, 1) for i in range(TOPK) if (i & d) == 0)


def _exchange(a, pairs):
    for i, j in pairs:
        a[i], a[j] = jnp.maximum(a[i], a[j]), jnp.minimum(a[i], a[j])


def _over_sublanes(x, op):
    for shift in (4, 2, 1):
        x = op(x, pltpu.roll(x, shift, 0))
    return x


def _sorted_top16(regs):
    a = list(regs)
    _exchange(a, SORT16)
    for shift in (4, 2, 1):
        b = [pltpu.roll(x, shift, 0) for x in a]
        a = [jnp.maximum(a[v], b[TOPK - 1 - v]) for v in range(TOPK)]
        _exchange(a, BITONIC16)
    return a


def _search16(s, v):
    c1 = s < v[7]
    c2 = s < jnp.where(c1, v[11], v[3])
    c3 = s < jnp.where(c1, jnp.where(c2, v[13], v[9]), jnp.where(c2, v[5], v[1]))
    c4 = s < jnp.where(c1, jnp.where(c2, jnp.where(c3, v[14], v[12]), jnp.where(c3, v[10], v[8])),
                       jnp.where(c2, jnp.where(c3, v[6], v[4]), jnp.where(c3, v[2], v[0])))
    return s < v[15], c1, c2, c3, c4


def _mux16(c, vals):
    _, c1, c2, c3, c4 = c
    lvl = [jnp.where(c4, vals[2 * k + 1], vals[2 * k]) for k in range(8)]
    lvl = [jnp.where(c3, lvl[2 * k + 1], lvl[2 * k]) for k in range(4)]
    lvl = [jnp.where(c2, lvl[2 * k + 1], lvl[2 * k]) for k in range(2)]
    return jnp.where(c1, lvl[1], lvl[0])


def _route_chunk_fast(s1, s2):
    nk, width = s1.shape
    assert nk == TOPK * SUBLANES
    regs = lambda a: [a[r:r + SUBLANES, :] for r in range(0, nk, SUBLANES)]
    s1r, s2r = regs(s1), regs(s2)
    v1 = _sorted_top16(s1r)
    v2 = _sorted_top16(s2r)
    sub = lax.broadcasted_iota(jnp.int32, (SUBLANES, width), 0)
    one = lambda m: jnp.where(m, 1.0, 0.0)

    def column(vs):
        out = vs[0]
        for k in range(1, SUBLANES):
            out = jnp.where(sub == k, vs[k], out)
        return out

    v2lo, v2hi, v1hi = column(v2[:SUBLANES]), column(v2[SUBLANES:]), column(v1[SUBLANES:])
    cands = [v1[0] + v2lo, v1[0] + v2hi] + [v1[i] + v2lo for i in range(1, SUBLANES)] + [v1hi + v2[0]]
    work = list(cands)
    treemax = lambda ws: _over_sublanes(functools.reduce(jnp.maximum, ws), jnp.maximum)
    for _ in range(TOPK - 1):
        m = treemax(work)
        work = [jnp.where(w == m, -jnp.inf, w) for w in work]
    tau = treemax(work)
    keep = [c >= tau for c in cands]
    kf = [one(k) for k in keep]
    cnt = [_over_sublanes(kf[0] + kf[1], jnp.add)]
    cnt += [_over_sublanes(kf[i + 1], jnp.add) for i in range(1, SUBLANES)]
    cnt += [jnp.broadcast_to(kf[-1][k:k + 1, :], kf[-1].shape) for k in range(SUBLANES)]
    top = v1[0] + v2[0]
    z = _over_sublanes(functools.reduce(jnp.add, [jnp.where(k, jnp.exp(c - top), 0.0)
                                                  for k, c in zip(keep, cands)]), jnp.add)
    inv_z = 1.0 / z

    bad = one(functools.reduce(jnp.add, cnt) != float(TOPK))
    for v in (v1, v2):
        for r in range(TOPK - 1):
            bad = jnp.maximum(bad, one(v[r] == v[r + 1]))

    r2, f2, cnt_key, f1 = [], [], [], []
    above1 = jnp.zeros((SUBLANES, width), F32)
    above2 = jnp.zeros((SUBLANES, width), F32)
    for s, t in zip(s1r, s2r):
        c = _search16(s, v1)
        cnt_key.append(jnp.where(c[0], 0.0, _mux16(c, cnt)))
        f1.append(jnp.exp(s - v1[0]) * inv_z)
        above1 = above1 + jnp.where(c[0], 0.0, 1.0)
        c = _search16(t, v2)
        rank = (jnp.where(c[1], 8.0, 0.0) + jnp.where(c[2], 4.0, 0.0)) + (jnp.where(c[3], 2.0, 0.0)
                                                                          + jnp.where(c[4], 1.0, 0.0))
        r2.append(jnp.where(c[0], NOT_SELECTED, rank))
        f2.append(jnp.exp(t - v2[0]))
        above2 = above2 + jnp.where(c[0], 0.0, 1.0)
    for above in (above1, above2):
        bad = jnp.maximum(bad, one(_over_sublanes(above, jnp.add) != float(TOPK)))
    cat = lambda rs: jnp.concatenate(rs, axis=0)
    return cat(r2), cat(f2), cat(cnt_key), cat(f1), bad


def _route_kernel(x2_ref, g2_ref, wqt_ref, k1_ref, k2_ref,
                  xnt_ref, r2_ref, f2_ref, cnt_ref, f1_ref, qt_ref):
    head = pl.program_id(1)
    nk = k1_ref.shape[0]
    half = k1_ref.shape[1]
    tb = x2_ref.shape[0]

    @pl.when(head == 0)
    def _():
        x = x2_ref[...]
        xn = x * _rms_scale(x) * g2_ref[...]
        xnt = xn.T.astype(BF16)
        xnt_ref[...] = xnt
        qt_ref[...] = jnp.dot(wqt_ref[...], xnt, preferred_element_type=F32).astype(BF16)

    base = pl.multiple_of(head * (2 * half), 2 * half)
    s1 = jnp.dot(k1_ref[...], qt_ref[pl.ds(base, half), :], preferred_element_type=F32)
    s2 = jnp.dot(k2_ref[...], qt_ref[pl.ds(base + half, half), :], preferred_element_type=F32)
    def store(cols, r2, f2, cnt_key, f1):
        r2_ref[:, cols] = r2.astype(BF16)
        f2_ref[:, cols] = f2.astype(BF16)
        cnt_ref[:, cols] = cnt_key
        f1_ref[:, cols] = f1

    chunks = [slice(c * LANES, (c + 1) * LANES) for c in range(tb // LANES)]
    bad = jnp.zeros((SUBLANES, LANES), F32)
    for cols in chunks:
        *tables, tie = _route_chunk_fast(s1[:, cols], s2[:, cols])
        store(cols, *tables)
        bad = jnp.maximum(bad, tie)

    @pl.when(jnp.max(bad) > 0.0)
    def _():
        for cols in chunks:
            store(cols, *_route_chunk(s1[:, cols], s2[:, cols]))
    del nk


def _route(x2, norm2_g, wq_t, k1, k2, *, tb, heads):
    n, d = x2.shape
    nk, half = k1.shape
    tok = lambda dt: jax.ShapeDtypeStruct((heads, nk, n), dt)
    per_head = pl.BlockSpec((None, nk, tb), lambda i, h: (h, 0, i))
    const = lambda a: pl.BlockSpec(a.shape, lambda i, h: (0,) * a.ndim)
    g2 = norm2_g.reshape(1, d).astype(F32)
    return pl.pallas_call(
        _route_kernel,
        out_shape=(jax.ShapeDtypeStruct((d, n), BF16), tok(BF16), tok(BF16), tok(F32), tok(F32)),
        grid=(n // tb, heads),
        in_specs=[pl.BlockSpec((tb, d), lambda i, h: (i, 0)), const(g2), const(wq_t), const(k1), const(k2)],
        out_specs=(pl.BlockSpec((d, tb), lambda i, h: (0, i)), per_head, per_head, per_head, per_head),
        scratch_shapes=[pltpu.VMEM((wq_t.shape[0], tb), BF16)],
        compiler_params=pltpu.CompilerParams(dimension_semantics=("arbitrary", "arbitrary"),
                                             vmem_limit_bytes=VMEM_LIMIT),
        name="route",
    )(x2, g2, wq_t, k1, k2)


def _peer_kernel(xnt_ref, u_ref, vt_ref, r2_ref, f2_ref, cnt_ref, f1_ref, x2_ref, gf_ref, o_ref,
                 acc_ref, at_ref, wt_ref):
    j = pl.program_id(1)
    heads, nk, tm = r2_ref.shape
    eb = u_ref.shape[0]
    blocks = eb // nk

    @pl.when(j == 0)
    def _():
        acc_ref[...] = jnp.zeros(acc_ref.shape, F32)

    at_ref[...] = jnp.dot(u_ref[...], xnt_ref[...], preferred_element_type=F32)
    for b in range(blocks):
        e1 = j * blocks + b
        rows = slice(b * nk, (b + 1) * nk)
        act = _gelu(at_ref[rows, :]).astype(BF16)
        tile_rows = 2 * SUBLANES
        tiled = (nk // tile_rows, tile_rows, tm)
        row = lambda ref, h: jnp.broadcast_to(ref[h, pl.ds(e1, 1), :], (tile_rows, tm)).astype(BF16)[None]
        gate = jnp.zeros(tiled, BF16)
        for h in range(heads):
            kept = jnp.where(r2_ref[h].reshape(tiled) < row(cnt_ref, h), f2_ref[h].reshape(tiled),
                             jnp.zeros((), BF16))
            gate = gate + kept * row(f1_ref, h)
        wt_ref[rows, :] = act * gate.reshape(nk, tm)
    acc_ref[...] += jnp.dot(vt_ref[...], wt_ref[...], preferred_element_type=F32)

    @pl.when(j == pl.num_programs(1) - 1)
    def _():
        y = x2_ref[...] + acc_ref[...].T
        o_ref[...] = y * _rms_scale(y) * gf_ref[...]


def _peer(xnt, u_bf, vt_bf, r2, f2, cnt, f1, x2, final_g, *, tm, eb):
    d, n = xnt.shape
    ne = u_bf.shape[0]
    heads, nk, _ = r2.shape
    gf = final_g.reshape(1, d).astype(F32)
    tables = pl.BlockSpec((heads, nk, tm), lambda i, j: (0, 0, i))
    return pl.pallas_call(
        _peer_kernel,
        out_shape=jax.ShapeDtypeStruct((n, d), F32),
        grid=(n // tm, ne // eb),
        in_specs=[pl.BlockSpec((d, tm), lambda i, j: (0, i)),
                  pl.BlockSpec((eb, d), lambda i, j: (j, 0)),
                  pl.BlockSpec((d, eb), lambda i, j: (0, j)),
                  tables, tables, tables, tables,
                  pl.BlockSpec((tm, d), lambda i, j: (i, 0)),
                  pl.BlockSpec((1, d), lambda i, j: (0, 0))],
        out_specs=pl.BlockSpec((tm, d), lambda i, j: (i, 0)),
        scratch_shapes=[pltpu.VMEM((d, tm), F32), pltpu.VMEM((eb, tm), F32), pltpu.VMEM((eb, tm), BF16)],
        compiler_params=pltpu.CompilerParams(dimension_semantics=("arbitrary", "arbitrary"),
                                             vmem_limit_bytes=VMEM_LIMIT),
        name="peer",
    )(xnt, u_bf, vt_bf, r2, f2, cnt, f1, x2, gf)


def _layer(x, norm1_g, w_in, lam_re, lam_im, log_dt, b_re, b_im, c_re, c_im, ssm_d, glu_w, glu_b,
           conv_w, conv_b, ln_g, ln_b, w_out, norm2_g, wq, k1, k2, u_tab, v_tab, out_g,
           *, tl, tb, tm, eb):
    bsz, seq, d = x.shape
    heads = wq.shape[1] // (2 * k1.shape[1])
    tables = _s5_tables(lam_re, lam_im, log_dt, b_re, b_im, c_re, c_im, glu_w)
    x2 = _mixer(x, norm1_g, w_in, tables, ssm_d, glu_b, conv_w, conv_b, ln_g, ln_b, w_out, tl=tl)
    x2 = x2.reshape(bsz * seq, d)
    xnt, r2, f2, cnt, f1 = _route(x2, norm2_g, wq.T.astype(BF16), k1.astype(BF16), k2.astype(BF16),
                                  tb=tb, heads=heads)
    out = _peer(xnt, u_tab.astype(BF16), v_tab.T.astype(BF16), r2, f2, cnt, f1, x2, out_g, tm=tm, eb=eb)
    return out.reshape(bsz, seq, d)


def kernel(x, norm1_g, w_in, lam_re, lam_im, log_dt, ssm_b_re, ssm_b_im, ssm_c_re, ssm_c_im, ssm_d, glu_w,
           glu_b, conv_w, conv_b, conv_ln_g, conv_ln_b, w_out, norm2_g, peer_wq, peer_k1, peer_k2, peer_u,
           peer_v, final_g):
    depth = w_in.shape[0]
    d = x.shape[-1]
    for l in range(depth):
        last = l == depth - 1
        assert last, "only the single-layer configuration is implemented"
        x = _layer(x, norm1_g[l], w_in[l], lam_re[l], lam_im[l], log_dt[l], ssm_b_re[l], ssm_b_im[l],
                   ssm_c_re[l], ssm_c_im[l], ssm_d[l], glu_w[l], glu_b[l], conv_w[l], conv_b[l],
                   conv_ln_g[l], conv_ln_b[l], w_out[l], norm2_g[l], peer_wq[l], peer_k1[l], peer_k2[l],
                   peer_u[l], peer_v[l], final_g, tl=256, tb=512, tm=512, eb=2048)
    del d
    return x
```
